```python
import math
import jax, jax.numpy as jnp
from jax import lax
import numpy as np

D_MODEL = 1024
BATCH = 16
SEQ = 4096
DEPTH = 1

PLE_DIM = 256
N_DIFF_HEADS = 4
DIFF_HEAD_DIM = 64
ATTN_WIDTH = N_DIFF_HEADS * 2 * DIFF_HEAD_DIM
POOL_WINDOWS = (2, 4, 8, 16)
N_POOL_GROUPS = len(POOL_WINDOWS)
POOL_GROUP_DIM = 128
POOL_WIDTH = N_POOL_GROUPS * POOL_GROUP_DIM
N_BRANCHES = 2
IN_WIDTH = 3 * ATTN_WIDTH + POOL_WIDTH + N_BRANCHES * D_MODEL
N_EXPERTS = 32
TOP_K = 4
D_FF = D_MODEL
SWIGLU_LIMIT = 7.0
SWIGLU_ALPHA = 1.702
Q_BLOCK = 128
MOE_BLOCK = 128
LN_EPS = 1e-5
DEEPNORM_ALPHA = (2 * DEPTH) ** 0.25
DEEPNORM_BETA = (8 * DEPTH) ** -0.25

kernel_name = 'hybrid_diffattn_pool_moe_deepnorm'

F32 = jnp.float32


def layer_norm(x, g, b):
    xf = x.astype(F32)
    mu = jnp.mean(xf, axis=-1, keepdims=True)
    var = jnp.mean(jnp.square(xf - mu), axis=-1, keepdims=True)
    return ((xf - mu) * lax.rsqrt(var + LN_EPS) * g + b).astype(x.dtype)


def rms_norm(x, g):
    xf = x.astype(F32)
    return (xf * lax.rsqrt(jnp.mean(jnp.square(xf), axis=-1, keepdims=True) + LN_EPS) * g).astype(x.dtype)


def alibi_slopes(n_heads):
    return jnp.asarray(np.array([2.0 ** (-8.0 * (h + 1) / n_heads) for h in range(n_heads)], dtype=np.float32))


def diff_attention(q, k, v, lam):
    Bb, Ss, H = q.shape[0], q.shape[1], q.shape[2]
    nq = Ss // Q_BLOCK
    qb = (q * (DIFF_HEAD_DIM ** -0.5)).reshape(Bb, nq, Q_BLOCK, H, 2, DIFF_HEAD_DIM).transpose(1, 0, 3, 4, 2, 5)
    kt = k.transpose(0, 2, 3, 1, 4)
    vt = v.transpose(0, 2, 1, 3)
    slopes = alibi_slopes(H)
    k_pos = jnp.arange(Ss)

    def block(args):
        q_blk, q_start = args
        s = jnp.einsum('bhmqd,bhmkd->bhmqk', q_blk, kt, preferred_element_type=F32)
        dist = (q_start + jnp.arange(Q_BLOCK))[:, None] - k_pos[None, :]
        bias = jnp.where(dist >= 0, -slopes[:, None, None] * dist.astype(F32), -jnp.inf)
        probs = jax.nn.softmax(s + bias[None, :, None], axis=-1)
        w = probs[:, :, 0] - lam * probs[:, :, 1]
        return jnp.einsum('bhqk,bhkv->bhqv', w.astype(vt.dtype), vt)

    starts = jnp.arange(nq) * Q_BLOCK
    o = lax.map(block, (qb, starts))
    return o.transpose(1, 0, 3, 2, 4).reshape(Bb, Ss, H, 2 * DIFF_HEAD_DIM)


def multiscale_pool(u, w_mix, scale):
    Bb, Ss, _ = u.shape
    ug = u.reshape(Bb, Ss, N_POOL_GROUPS, POOL_GROUP_DIM)
    cs = jnp.cumsum(ug.astype(F32), axis=1)
    pos = jnp.arange(Ss)
    means = []
    for gi, w in enumerate(POOL_WINDOWS):
        c = cs[:, :, gi]
        lagged = jnp.pad(c, ((0, 0), (w, 0), (0, 0)))[:, :Ss]
        count = jnp.minimum(pos + 1, w).astype(F32)[None, :, None]
        means.append((c - lagged) / count)
    pooled = (jnp.stack(means, axis=2) - ug.astype(F32)).astype(u.dtype)
    mixed = jnp.einsum('bsgc,gcd->bsgd', pooled, w_mix).reshape(Bb, Ss, POOL_WIDTH)
    return mixed * scale


def clamped_swiglu(h):
    glu, lin = jnp.split(h, 2, axis=-1)
    glu = jnp.minimum(glu, SWIGLU_LIMIT)
    lin = jnp.clip(lin, -SWIGLU_LIMIT, SWIGLU_LIMIT)
    return glu * jax.nn.sigmoid(SWIGLU_ALPHA * glu) * (lin + 1.0)


def moe(h, w_router, b_router, w_up, b_up, w_down, b_down):
    Bb, Ss, D = h.shape
    T = Bb * Ss
    A = T * TOP_K
    t = h.reshape(T, D)
    logits = (t @ w_router + b_router).astype(F32)
    top_val, top_idx = lax.top_k(logits, TOP_K)
    gate = jax.nn.softmax(top_val, axis=-1).reshape(A)
    flat_e = top_idx.reshape(A)
    flat_tok = jnp.arange(A) // TOP_K
    order = jnp.argsort(flat_e)
    sorted_e = flat_e[order]
    counts = jnp.bincount(flat_e, length=N_EXPERTS)
    padded = ((counts + MOE_BLOCK - 1) // MOE_BLOCK) * MOE_BLOCK
    cum_padded = jnp.cumsum(padded)
    pad_start = cum_padded - padded
    start = jnp.cumsum(counts) - counts
    dest = pad_start[sorted_e] + (jnp.arange(A) - start[sorted_e])
    n_slots = A + N_EXPERTS * MOE_BLOCK
    n_blocks = n_slots // MOE_BLOCK
    slot_tok = jnp.full((n_slots,), T, jnp.int32).at[dest].set(flat_tok[order])
    slot_gate = jnp.zeros((n_slots,), F32).at[dest].set(gate[order])
    block_expert = jnp.minimum(jnp.searchsorted(cum_padded, jnp.arange(n_blocks) * MOE_BLOCK, side='right'), N_EXPERTS - 1)
    t_pad = jnp.concatenate([t, jnp.zeros((1, D), t.dtype)], axis=0)
    xs = t_pad[slot_tok].reshape(n_blocks, MOE_BLOCK, D)

    def expert_block(args):
        xb, e = args
        hb = clamped_swiglu(xb @ w_up[e] + b_up[e])
        return hb @ w_down[e] + b_down[e]

    ys = lax.map(expert_block, (xs, block_expert)).reshape(n_slots, D)
    out = jnp.zeros((T + 1, D), ys.dtype).at[slot_tok].add(ys * slot_gate[:, None].astype(ys.dtype))
    return out[:T].reshape(Bb, Ss, D)


def setup_inputs(seed: int = 0) -> dict:
    key = jax.random.key(seed)
    ks = jax.random.split(key, 32)
    L = DEPTH

    def nrm(k, shape, scale):
        return jax.random.normal(k, shape, F32) * scale

    s_in = D_MODEL ** -0.5
    w_qk = nrm(ks[2], (L, D_MODEL, 2 * ATTN_WIDTH), s_in)
    w_v = nrm(ks[3], (L, D_MODEL, ATTN_WIDTH), s_in * DEEPNORM_BETA)
    w_pool_in = nrm(ks[4], (L, D_MODEL, POOL_WIDTH), s_in)
    w_gate = nrm(ks[5], (L, D_MODEL, N_BRANCHES * D_MODEL), s_in)
    return {
        'x': nrm(ks[0], (BATCH, SEQ, D_MODEL), 1.0),
        'p': nrm(ks[1], (DEPTH, BATCH, SEQ, PLE_DIM), 1.0),
        'w_in': jnp.concatenate([w_qk, w_v, w_pool_in, w_gate], axis=-1),
        'b_gate': nrm(ks[6], (L, N_BRANCHES * D_MODEL), 0.02),
        'lambda_q1': nrm(ks[7], (L, DIFF_HEAD_DIM), 0.1),
        'lambda_k1': nrm(ks[8], (L, DIFF_HEAD_DIM), 0.1),
        'lambda_q2': nrm(ks[9], (L, DIFF_HEAD_DIM), 0.1),
        'lambda_k2': nrm(ks[10], (L, DIFF_HEAD_DIM), 0.1),
        'subln_g': 1.0 + nrm(ks[11], (L, 2 * DIFF_HEAD_DIM), 0.02),
        'w_attn_br': nrm(ks[12], (L, ATTN_WIDTH, D_MODEL), ATTN_WIDTH ** -0.5),
        'w_pool_mix': nrm(ks[13], (L, N_POOL_GROUPS, POOL_GROUP_DIM, POOL_GROUP_DIM), POOL_GROUP_DIM ** -0.5),
        'pool_scale': 1.0 + nrm(ks[14], (L, POOL_WIDTH), 0.02),
        'w_pool_br': nrm(ks[15], (L, POOL_WIDTH, D_MODEL), POOL_WIDTH ** -0.5),
        'w_out': nrm(ks[16], (L, D_MODEL, D_MODEL), s_in * DEEPNORM_BETA),
        'ln1_g': 1.0 + nrm(ks[17], (L, D_MODEL), 0.02),
        'ln1_b': nrm(ks[18], (L, D_MODEL), 0.02),
        'w_router': nrm(ks[19], (L, D_MODEL, N_EXPERTS), s_in),
        'b_router': nrm(ks[20], (L, N_EXPERTS), 0.01),
        'w_up': nrm(ks[21], (L, N_EXPERTS, D_MODEL, 2 * D_FF), s_in),
        'b_up': nrm(ks[22], (L, N_EXPERTS, 2 * D_FF), 0.02),
        'w_down': nrm(ks[23], (L, N_EXPERTS, D_FF, D_MODEL), D_FF ** -0.5 * DEEPNORM_BETA),
        'b_down': nrm(ks[24], (L, N_EXPERTS, D_MODEL), 0.02),
        'w_ple_gate': nrm(ks[25], (L, D_MODEL, D_MODEL), s_in),
        'w_ple_proj': nrm(ks[26], (L, PLE_DIM, D_MODEL), PLE_DIM ** -0.5 * DEEPNORM_BETA),
        'ln2_g': 1.0 + nrm(ks[27], (L, D_MODEL), 0.02),
        'ln2_b': nrm(ks[28], (L, D_MODEL), 0.02),
    }


def reference(x, p, w_in, b_gate, lambda_q1, lambda_k1, lambda_q2, lambda_k2, subln_g, w_attn_br,
              w_pool_mix, pool_scale, w_pool_br, w_out, ln1_g, ln1_b, w_router, b_router, w_up, b_up,
              w_down, b_down, w_ple_gate, w_ple_proj, ln2_g, ln2_b):
    Bb, Ss, _ = x.shape
    splits = [ATTN_WIDTH, 2 * ATTN_WIDTH, 3 * ATTN_WIDTH, 3 * ATTN_WIDTH + POOL_WIDTH]
    for i in range(DEPTH):
        proj = x @ w_in[i]
        q, k, v, u, g = jnp.split(proj, splits, axis=-1)
        q = q.reshape(Bb, Ss, N_DIFF_HEADS, 2, DIFF_HEAD_DIM)
        k = k.reshape(Bb, Ss, N_DIFF_HEADS, 2, DIFF_HEAD_DIM)
        v = v.reshape(Bb, Ss, N_DIFF_HEADS, 2 * DIFF_HEAD_DIM)
        lam_init = 0.8 - 0.6 * math.exp(-0.3 * i)
        lam = (jnp.exp(jnp.sum(lambda_q1[i].astype(F32) * lambda_k1[i].astype(F32)))
               - jnp.exp(jnp.sum(lambda_q2[i].astype(F32) * lambda_k2[i].astype(F32))) + lam_init)
        o = diff_attention(q, k, v, lam)
        o = rms_norm(o, subln_g[i]) * (1.0 - lam_init)
        a_branch = o.reshape(Bb, Ss, ATTN_WIDTH) @ w_attn_br[i]
        p_branch = multiscale_pool(u, w_pool_mix[i], pool_scale[i]) @ w_pool_br[i]
        g_a, g_p = jnp.split(jax.nn.sigmoid(g + b_gate[i]), 2, axis=-1)
        mixed = g_a * a_branch + g_p * p_branch
        x = layer_norm(DEEPNORM_ALPHA * x + mixed @ w_out[i], ln1_g[i], ln1_b[i])
        ffn = moe(x, w_router[i], b_router[i], w_up[i], b_up[i], w_down[i], b_down[i])
        ple = jax.nn.sigmoid(x @ w_ple_gate[i]) * (p[i] @ w_ple_proj[i])
        x = layer_norm(DEEPNORM_ALPHA * x + ffn + ple, ln2_g[i], ln2_b[i])
    return x
```

```python
import functools
import math

import jax
import jax.numpy as jnp
from jax import lax
from jax.experimental import pallas as pl
from jax.experimental.pallas import tpu as pltpu

F32 = jnp.float32
BF16 = jnp.bfloat16
I32 = jnp.int32

N_DIFF_HEADS = 4
DIFF_HEAD_DIM = 64
HEAD_WIDTH = 2 * DIFF_HEAD_DIM
ATTN_WIDTH = N_DIFF_HEADS * HEAD_WIDTH
POOL_WINDOWS = (2, 4, 8, 16)
POOL_GROUP_DIM = 128
POOL_WIDTH = len(POOL_WINDOWS) * POOL_GROUP_DIM
POOL_HALO = 16
N_EXPERTS = 32
TOP_K = 4
SWIGLU_LIMIT = 7.0
SWIGLU_ALPHA = 1.702
LN_EPS = 1e-5
LANES = 128

TOKEN_TILE = 512
ATTN_TILE = 512
EXPERT_BLOCK = 512
VMEM_LIMIT = 56 * 1024 * 1024


def _sigmoid(z):
    return 1.0 / (1.0 + jnp.exp(-z))


def _layer_norm(z, g, b):
    mu = jnp.mean(z, axis=-1, keepdims=True)
    zc = z - mu
    var = jnp.mean(zc * zc, axis=-1, keepdims=True)
    return zc * lax.rsqrt(var + LN_EPS) * g + b


def _params(n_axes=1):
    return pltpu.CompilerParams(dimension_semantics=("arbitrary",) * n_axes,
                                vmem_limit_bytes=VMEM_LIMIT)


def _inproj_kernel(x_ref, win_ref, bg_ref, wmix_ref, pscale_ref, wpbr_ref,
                   q_ref, k_ref, v_ref, ga_ref, gp_ref, ubuf, *, tiles_per_seq, tm, d_model):
    j = pl.program_id(0) % tiles_per_seq
    xb = x_ref[...].astype(BF16)

    def proj(lo, hi):
        return jnp.dot(xb, win_ref[:, lo:hi], preferred_element_type=F32)

    aw = ATTN_WIDTH
    q_ref[...] = (proj(0, aw) * (DIFF_HEAD_DIM ** -0.5)).astype(BF16)
    k_ref[...] = proj(aw, 2 * aw).astype(BF16)
    v_ref[...] = proj(2 * aw, 3 * aw).astype(BF16)
    u = proj(3 * aw, 3 * aw + POOL_WIDTH)

    @pl.when(j == 0)
    def _():
        ubuf[0:POOL_HALO, :] = jnp.zeros((POOL_HALO, POOL_WIDTH), F32)

    ubuf[POOL_HALO:, :] = u
    pos = j * tm + lax.broadcasted_iota(I32, (tm, 1), 0)
    parts = []
    for gi, w in enumerate(POOL_WINDOWS):
        lo, hi = gi * POOL_GROUP_DIM, (gi + 1) * POOL_GROUP_DIM
        a = ubuf[:, lo:hi]
        s = a
        sh = 1
        while sh < w:
            s = s + pltpu.roll(s, sh, axis=0)
            sh *= 2
        cnt = jnp.minimum(pos + 1, w).astype(F32)
        pooled = s[POOL_HALO:, :] / cnt - a[POOL_HALO:, :]
        mixed = jnp.dot(pooled.astype(BF16), wmix_ref[gi], preferred_element_type=F32)
        parts.append(mixed * pscale_ref[:, lo:hi])
    ubuf[0:POOL_HALO, :] = u[tm - POOL_HALO:, :]
    p_branch = jnp.dot(jnp.concatenate(parts, axis=-1).astype(BF16), wpbr_ref[...],
                       preferred_element_type=F32)

    g0 = 3 * aw + POOL_WIDTH
    g_a = _sigmoid(proj(g0, g0 + d_model) + bg_ref[:, 0:d_model])
    ga_ref[...] = g_a.astype(BF16)
    g_p = _sigmoid(proj(g0 + d_model, g0 + 2 * d_model) + bg_ref[:, d_model:2 * d_model])
    gp_ref[...] = (g_p * p_branch).astype(BF16)


def _inproj(x2, w_in, b_gate, w_mix, pool_scale, w_pool_br, seq):
    t, d = x2.shape
    tm = TOKEN_TILE
    in_w = w_in.shape[1]
    row = lambda i: (i, 0)
    full2 = lambda i: (0, 0)
    kern = functools.partial(_inproj_kernel, tiles_per_seq=seq // tm, tm=tm, d_model=d)
    return pl.pallas_call(
        kern,
        grid=(t // tm,),
        in_specs=[
            pl.BlockSpec((tm, d), row),
            pl.BlockSpec((d, in_w), full2),
            pl.BlockSpec((1, 2 * d), full2),
            pl.BlockSpec(w_mix.shape, lambda i: (0, 0, 0)),
            pl.BlockSpec((1, POOL_WIDTH), full2),
            pl.BlockSpec((POOL_WIDTH, d), full2),
        ],
        out_specs=[
            pl.BlockSpec((tm, ATTN_WIDTH), row),
            pl.BlockSpec((tm, ATTN_WIDTH), row),
            pl.BlockSpec((tm, ATTN_WIDTH), row),
            pl.BlockSpec((tm, d), row),
            pl.BlockSpec((tm, d), row),
        ],
        out_shape=[
            jax.ShapeDtypeStruct((t, ATTN_WIDTH), BF16),
            jax.ShapeDtypeStruct((t, ATTN_WIDTH), BF16),
            jax.ShapeDtypeStruct((t, ATTN_WIDTH), BF16),
            jax.ShapeDtypeStruct((t, d), BF16),
            jax.ShapeDtypeStruct((t, d), BF16),
        ],
        scratch_shapes=[pltpu.VMEM((POOL_HALO + tm, POOL_WIDTH), F32)],
        compiler_params=_params(),
        name="inproj",
    )(x2, w_in, b_gate, w_mix, pool_scale, w_pool_br)


def _attn_kernel(slopes_ref, lvec_ref, sg_ref, q_ref, k_ref, v_ref, o_ref, m_scr, l_scr, acc_scr,
                 *, tq, lam_init):
    h = pl.program_id(1)
    qi = pl.program_id(2)
    slope = slopes_ref[h]
    q = q_ref[...]
    lane = lax.broadcasted_iota(I32, q.shape, 1)
    zero = jnp.zeros_like(q)
    qm = (jnp.where(lane < DIFF_HEAD_DIM, q, zero), jnp.where(lane >= DIFF_HEAD_DIM, q, zero))

    m_scr[...] = jnp.full(m_scr.shape, -jnp.inf, F32)
    l_scr[...] = jnp.zeros(l_scr.shape, F32)
    acc_scr[...] = jnp.zeros(acc_scr.shape, F32)

    col = lax.broadcasted_iota(I32, (1, tq), 1)

    def step(ki, masked):
        start = pl.multiple_of(ki * tq, tq)
        k = k_ref[pl.ds(start, tq), :]
        v = v_ref[pl.ds(start, tq), :]
        cb = slope * (col + (ki - qi) * tq).astype(F32)
        for m in range(2):
            s = lax.dot_general(qm[m], k, (((1,), (1,)), ((), ())), preferred_element_type=F32)
            z = s + cb
            if masked:
                r = lax.broadcasted_iota(I32, z.shape, 0)
                c = lax.broadcasted_iota(I32, z.shape, 1)
                z = jnp.where(c <= r, z, -jnp.inf)
            m_old = m_scr[m]
            m_new = jnp.maximum(m_old, jnp.max(z, axis=-1, keepdims=True))
            alpha = jnp.exp(m_old - m_new)
            p = jnp.exp(z - m_new)
            l_scr[m] = alpha * l_scr[m] + jnp.sum(p, axis=-1, keepdims=True)
            acc_scr[m] = alpha * acc_scr[m] + jnp.dot(p.astype(BF16), v, preferred_element_type=F32)
            m_scr[m] = m_new

    def body(ki, carry):
        step(ki, False)
        return carry

    lax.fori_loop(0, qi, body, 0)
    step(qi, True)

    lv = lvec_ref[...]
    lam = (jnp.exp(jnp.sum(lv[0:1] * lv[1:2], axis=-1, keepdims=True))
           - jnp.exp(jnp.sum(lv[2:3] * lv[3:4], axis=-1, keepdims=True)) + lam_init)
    o = acc_scr[0] / l_scr[0] - lam * (acc_scr[1] / l_scr[1])
    ms = jnp.mean(o * o, axis=-1, keepdims=True)
    o_ref[...] = (o * lax.rsqrt(ms + LN_EPS) * sg_ref[...] * (1.0 - lam_init)).astype(BF16)


def _attention(q, k, v, slopes, lvec, subln_g, lam_init):
    b, s, _ = q.shape
    tq = ATTN_TILE
    kern = functools.partial(_attn_kernel, tq=tq, lam_init=lam_init)
    qspec = pl.BlockSpec((None, tq, HEAD_WIDTH), lambda bi, h, i: (bi, i, h))
    kvspec = pl.BlockSpec((None, s, HEAD_WIDTH), lambda bi, h, i: (bi, 0, h))
    return pl.pallas_call(
        kern,
        grid=(b, N_DIFF_HEADS, s // tq),
        in_specs=[
            pl.BlockSpec(memory_space=pltpu.SMEM),
            pl.BlockSpec(lvec.shape, lambda bi, h, i: (0, 0)),
            pl.BlockSpec((1, HEAD_WIDTH), lambda bi, h, i: (0, 0)),
            qspec, kvspec, kvspec,
        ],
        out_specs=qspec,
        out_shape=jax.ShapeDtypeStruct((b, s, ATTN_WIDTH), BF16),
        scratch_shapes=[
            pltpu.VMEM((2, tq, 1), F32),
            pltpu.VMEM((2, tq, 1), F32),
            pltpu.VMEM((2, tq, HEAD_WIDTH), F32),
        ],
        compiler_params=_params(3),
        name="diffattn",
    )(slopes, lvec, subln_g, q, k, v)


def _mix_kernel(o_ref, ga_ref, gp_ref, x_ref, wabr_ref, wout_ref, g1_ref, b1_ref, wr_ref, br_ref,
                x1_ref, topi_ref, rank_ref, gate_ref, cnt_ref, carry, *, tm, alpha):
    i = pl.program_id(0)

    @pl.when(i == 0)
    def _():
        carry[...] = jnp.zeros(carry.shape, F32)

    a_branch = jnp.dot(o_ref[...], wabr_ref[...], preferred_element_type=F32)
    mixed = ga_ref[...].astype(F32) * a_branch + gp_ref[...].astype(F32)
    y = jnp.dot(mixed.astype(BF16), wout_ref[...], preferred_element_type=F32)
    x1 = _layer_norm(alpha * x_ref[...] + y, g1_ref[...], b1_ref[...])
    x1_ref[...] = x1

    logits = jnp.dot(x1, wr_ref[...], precision=lax.Precision.HIGHEST,
                     preferred_element_type=F32) + br_ref[...]
    e_iota = lax.broadcasted_iota(I32, logits.shape, 1).astype(F32)
    vals, idxs = [], []
    rem = logits
    for _ in range(TOP_K):
        m = jnp.max(rem, axis=-1, keepdims=True)
        idx = jnp.min(jnp.where(rem == m, e_iota, float(N_EXPERTS)), axis=-1, keepdims=True)
        vals.append(m)
        idxs.append(idx)
        rem = jnp.where(e_iota == idx, -jnp.inf, rem)
    exps = [jnp.exp(vk - vals[0]) for vk in vals]
    den = exps[0] + exps[1] + exps[2] + exps[3]

    sel = jnp.zeros(logits.shape, F32)
    for idx in idxs:
        sel = sel + jnp.where(e_iota == idx, 1.0, 0.0)
    r = lax.broadcasted_iota(I32, (tm, tm), 0)
    c = lax.broadcasted_iota(I32, (tm, tm), 1)
    tri = jnp.where(c < r, 1.0, 0.0).astype(BF16)
    prefix = jnp.dot(tri, sel.astype(BF16), preferred_element_type=F32) + carry[...]
    carry[...] = carry[...] + jnp.sum(sel, axis=0, keepdims=True)
    cnt_ref[...] = carry[...]

    lane = lax.broadcasted_iota(I32, (tm, LANES), 1)
    topi = jnp.zeros((tm, LANES), F32)
    rank = jnp.zeros((tm, LANES), F32)
    gate = jnp.zeros((tm, LANES), F32)
    for kk in range(TOP_K):
        rk = jnp.sum(jnp.where(e_iota == idxs[kk], prefix, 0.0), axis=-1, keepdims=True)
        topi = jnp.where(lane == kk, idxs[kk], topi)
        rank = jnp.where(lane == kk, rk, rank)
        gate = jnp.where(lane == kk, exps[kk] / den, gate)
    topi_ref[...] = topi.astype(I32)
    rank_ref[...] = rank.astype(I32)
    gate_ref[...] = gate


def _mix(o2, ga, gp, x2, w_attn_br, w_out, ln_g, ln_b, w_router, b_router, alpha):
    t, d = x2.shape
    tm = TOKEN_TILE
    row = lambda i: (i, 0)
    full2 = lambda i: (0, 0)
    kern = functools.partial(_mix_kernel, tm=tm, alpha=alpha)
    return pl.pallas_call(
        kern,
        grid=(t // tm,),
        in_specs=[
            pl.BlockSpec((tm, ATTN_WIDTH), row),
            pl.BlockSpec((tm, d), row),
            pl.BlockSpec((tm, d), row),
            pl.BlockSpec((tm, d), row),
            pl.BlockSpec((ATTN_WIDTH, d), full2),
            pl.BlockSpec((d, d), full2),
            pl.BlockSpec((1, d), full2),
            pl.BlockSpec((1, d), full2),
            pl.BlockSpec((d, N_EXPERTS), full2),
            pl.BlockSpec((1, N_EXPERTS), full2),
        ],
        out_specs=[
            pl.BlockSpec((tm, d), row),
            pl.BlockSpec((tm, LANES), row),
            pl.BlockSpec((tm, LANES), row),
            pl.BlockSpec((tm, LANES), row),
            pl.BlockSpec((1, N_EXPERTS), full2),
        ],
        out_shape=[
            jax.ShapeDtypeStruct((t, d), F32),
            jax.ShapeDtypeStruct((t, LANES), I32),
            jax.ShapeDtypeStruct((t, LANES), I32),
            jax.ShapeDtypeStruct((t, LANES), F32),
            jax.ShapeDtypeStruct((1, N_EXPERTS), F32),
        ],
        scratch_shapes=[pltpu.VMEM((1, N_EXPERTS), F32)],
        compiler_params=_params(),
        name="mix_ln1_router",
    )(o2, ga, gp, x2, w_attn_br, w_out, ln_g, ln_b, w_router, b_router)


def _row_copy(src_hbm, src_row, dst_ref, dst_row, sem):
    return pltpu.make_async_copy(src_hbm.at[pl.ds(src_row, 1), :], dst_ref.at[pl.ds(dst_row, 1), :], sem)


def _dispatch_kernel(dest_ref, x1_hbm, xs_hbm, sem, *, tm):
    base = pl.program_id(0) * tm

    def body(t, carry):
        for kk in range(TOP_K):
            _row_copy(x1_hbm, base + t, xs_hbm, dest_ref[TOP_K * t + kk], sem).start()
        return carry

    lax.fori_loop(0, tm, body, 0)
    n = TOP_K * tm
    pltpu.make_async_copy(x1_hbm.at[pl.ds(0, n), :], xs_hbm.at[pl.ds(0, n), :], sem).wait()


def _dispatch(dest, x1, n_slots):
    t, d = x1.shape
    tm = TOKEN_TILE
    kern = functools.partial(_dispatch_kernel, tm=tm)
    return pl.pallas_call(
        kern,
        grid=(t // tm,),
        in_specs=[
            pl.BlockSpec((TOP_K * tm,), lambda i: (i,), memory_space=pltpu.SMEM),
            pl.BlockSpec(memory_space=pl.ANY),
        ],
        out_specs=pl.BlockSpec(memory_space=pl.ANY),
        out_shape=jax.ShapeDtypeStruct((n_slots, d), F32),
        scratch_shapes=[pltpu.SemaphoreType.DMA(())],
        compiler_params=_params(),
        name="dispatch",
    )(dest, x1)


def _expert_kernel(be_ref, bsrc_ref, bv_ref, xs_ref, wup_ref, bup_ref, wdn_ref, bdn_ref, ys_ref,
                   *, bm, d_ff):
    nv = bv_ref[pl.program_id(0)]

    @pl.when(nv > 0)
    def _():
        rows = lax.broadcasted_iota(I32, (bm, 1), 0)
        x = jnp.where(rows < nv, xs_ref[...], 0.0).astype(BF16)
        hid = jnp.dot(x, wup_ref[0], preferred_element_type=F32) + bup_ref[0]
        glu = jnp.minimum(hid[:, :d_ff], SWIGLU_LIMIT)
        lin = jnp.clip(hid[:, d_ff:], -SWIGLU_LIMIT, SWIGLU_LIMIT)
        act = glu * _sigmoid(SWIGLU_ALPHA * glu) * (lin + 1.0)
        ys_ref[...] = jnp.dot(act.astype(BF16), wdn_ref[0], preferred_element_type=F32) + bdn_ref[0]


def _experts(be, bsrc, bv, xs, w_up, b_up, w_down, b_down):
    n_slots, d = xs.shape
    bm = EXPERT_BLOCK
    d_ff = w_down.shape[1]
    kern = functools.partial(_expert_kernel, bm=bm, d_ff=d_ff)
    blk = lambda b, be_r, bsrc_r, bv_r: (bsrc_r[b], 0)
    wsel = lambda b, be_r, bsrc_r, bv_r: (be_r[b], 0, 0)
    grid_spec = pltpu.PrefetchScalarGridSpec(
        num_scalar_prefetch=3,
        grid=(n_slots // bm,),
        in_specs=[
            pl.BlockSpec((bm, d), blk),
            pl.BlockSpec((1, d, 2 * d_ff), wsel),
            pl.BlockSpec((1, 1, 2 * d_ff), wsel),
            pl.BlockSpec((1, d_ff, d), wsel),
            pl.BlockSpec((1, 1, d), wsel),
        ],
        out_specs=pl.BlockSpec((bm, d), blk),
    )
    return pl.pallas_call(
        kern,
        grid_spec=grid_spec,
        out_shape=jax.ShapeDtypeStruct((n_slots, d), F32),
        compiler_params=_params(),
        name="experts",
    )(be, bsrc, bv, xs, w_up, b_up, w_down, b_down)


def _final_kernel(dest_ref, x1_ref, p_ref, gate_ref, wg_ref, wp_ref, g2_ref, b2_ref, ys_hbm,
                  o_ref, ybuf, sem, *, tm, alpha):
    def body(t, carry):
        for kk in range(TOP_K):
            _row_copy(ys_hbm, dest_ref[TOP_K * t + kk], ybuf.at[kk], t, sem).start()
        return carry

    lax.fori_loop(0, tm, body, 0)

    x1 = x1_ref[...]
    ple = (_sigmoid(jnp.dot(x1.astype(BF16), wg_ref[...], preferred_element_type=F32))
           * jnp.dot(p_ref[...].astype(BF16), wp_ref[...], preferred_element_type=F32))

    for kk in range(TOP_K):
        pltpu.make_async_copy(ys_hbm.at[pl.ds(0, tm), :], ybuf.at[kk], sem).wait()
    gate = gate_ref[...]
    ffn = gate[:, 0:1] * ybuf[0]
    for kk in range(1, TOP_K):
        ffn = ffn + gate[:, kk:kk + 1] * ybuf[kk]
    o_ref[...] = _layer_norm(alpha * x1 + ffn + ple, g2_ref[...], b2_ref[...])


def _final(dest, x1, p2, gate, w_ple_gate, w_ple_proj, ln_g, ln_b, ys, alpha):
    t, d = x1.shape
    tm = TOKEN_TILE
    ple_dim = p2.shape[1]
    row = lambda i: (i, 0)
    full2 = lambda i: (0, 0)
    kern = functools.partial(_final_kernel, tm=tm, alpha=alpha)
    return pl.pallas_call(
        kern,
        grid=(t // tm,),
        in_specs=[
            pl.BlockSpec((TOP_K * tm,), lambda i: (i,), memory_space=pltpu.SMEM),
            pl.BlockSpec((tm, d), row),
            pl.BlockSpec((tm, ple_dim), row),
            pl.BlockSpec((tm, LANES), row),
            pl.BlockSpec((d, d), full2),
            pl.BlockSpec((ple_dim, d), full2),
            pl.BlockSpec((1, d), full2),
            pl.BlockSpec((1, d), full2),
            pl.BlockSpec(memory_space=pl.ANY),
        ],
        out_specs=pl.BlockSpec((tm, d), row),
        out_shape=jax.ShapeDtypeStruct((t, d), F32),
        scratch_shapes=[pltpu.VMEM((TOP_K, tm, d), F32), pltpu.SemaphoreType.DMA(())],
        compiler_params=_params(),
        name="combine_ple_ln2",
    )(dest, x1, p2, gate, w_ple_gate, w_ple_proj, ln_g, ln_b, ys)


def _routing_tables(counts, topi, rank, n_blocks):
    bm = EXPERT_BLOCK
    n_blk_e = (counts + bm - 1) // bm
    cum_blk = jnp.cumsum(n_blk_e)
    blk_start = cum_blk - n_blk_e
    n_used = cum_blk[-1]
    experts = jnp.arange(N_EXPERTS, dtype=I32)
    slot_start = jnp.sum(jnp.where(topi[..., None] == experts, blk_start * bm, 0), axis=-1)
    dest = (slot_start + rank).reshape(-1).astype(I32)
    bidx = jnp.arange(n_blocks, dtype=I32)
    bsrc = jnp.minimum(bidx, n_used - 1).astype(I32)
    be = jnp.minimum(jnp.searchsorted(cum_blk, bsrc, side="right"), N_EXPERTS - 1).astype(I32)
    valid = jnp.clip(counts[be] - (bsrc - blk_start[be]) * bm, 0, bm)
    bv = jnp.where(bidx < n_used, valid, 0).astype(I32)
    return dest, be, bsrc, bv


def kernel(x, p, w_in, b_gate, lambda_q1, lambda_k1, lambda_q2, lambda_k2, subln_g, w_attn_br,
           w_pool_mix, pool_scale, w_pool_br, w_out, ln1_g, ln1_b, w_router, b_router, w_up, b_up,
           w_down, b_down, w_ple_gate, w_ple_proj, ln2_g, ln2_b):
    bsz, seq, d = x.shape
    depth = w_in.shape[0]
    t = bsz * seq
    assert seq % ATTN_TILE == 0 and seq % TOKEN_TILE == 0 and t % EXPERT_BLOCK == 0
    alpha = (2 * depth) ** 0.25
    n_blocks = (t * TOP_K) // EXPERT_BLOCK + N_EXPERTS
    n_slots = n_blocks * EXPERT_BLOCK
    slopes = jnp.asarray([2.0 ** (-8.0 * (h + 1) / N_DIFF_HEADS) for h in range(N_DIFF_HEADS)], F32)

    x2 = x.reshape(t, d)
    for i in range(depth):
        lam_init = 0.8 - 0.6 * math.exp(-0.3 * i)
        q, k, v, ga, gp = _inproj(x2, w_in[i].astype(BF16), b_gate[i][None], w_pool_mix[i].astype(BF16),
                                  pool_scale[i][None], w_pool_br[i].astype(BF16), seq)
        lvec = jnp.stack([lambda_q1[i], lambda_k1[i], lambda_q2[i], lambda_k2[i]]).astype(F32)
        shp = (bsz, seq, ATTN_WIDTH)
        o = _attention(q.reshape(shp), k.reshape(shp), v.reshape(shp), slopes, lvec,
                       subln_g[i][None], lam_init)
        x1, topi, rank, gate, cnt = _mix(o.reshape(t, ATTN_WIDTH), ga, gp, x2,
                                         w_attn_br[i].astype(BF16), w_out[i].astype(BF16),
                                         ln1_g[i][None], ln1_b[i][None], w_router[i], b_router[i][None],
                                         alpha)
        counts = cnt[0].astype(I32)
        dest, be, bsrc, bv = _routing_tables(counts, topi[:, :TOP_K], rank[:, :TOP_K], n_blocks)
        xs = _dispatch(dest, x1, n_slots)
        ys = _experts(be, bsrc, bv, xs, w_up[i].astype(BF16), b_up[i][:, None, :],
                      w_down[i].astype(BF16), b_down[i][:, None, :])
        x2 = _final(dest, x1, p[i].reshape(t, -1), gate, w_ple_gate[i].astype(BF16),
                    w_ple_proj[i].astype(BF16), ln2_g[i][None], ln2_b[i][None], ys, alpha)
    return x2.reshape(bsz, seq, d)
```

```python
import functools
import math

import jax
import jax.numpy as jnp
import numpy as np
from jax import lax
from jax.experimental import pallas as pl
from jax.experimental.pallas import tpu as pltpu

F32 = jnp.float32
BF16 = jnp.bfloat16
I32 = jnp.int32

N_DIFF_HEADS = 4
DIFF_HEAD_DIM = 64
HEAD_WIDTH = 2 * DIFF_HEAD_DIM
ATTN_WIDTH = N_DIFF_HEADS * HEAD_WIDTH
POOL_WINDOWS = (2, 4, 8, 16)
POOL_GROUP_DIM = 128
POOL_WIDTH = len(POOL_WINDOWS) * POOL_GROUP_DIM
POOL_HALO = 16
N_EXPERTS = 32
TOP_K = 4
SWIGLU_LIMIT = 7.0
SWIGLU_ALPHA = 1.702
LN_EPS = 1e-5
LANES = 128
LOG2E = math.log2(math.e)
N_BIAS_PIECES = 3
IN_SPLITS = (ATTN_WIDTH, 2 * ATTN_WIDTH, 3 * ATTN_WIDTH, 3 * ATTN_WIDTH + POOL_WIDTH)

TOKEN_TILE = 512
ATTN_TILE = 512
QUERY_CHUNK = 512
EXPERT_BLOCK = 512
VMEM_LIMIT = 56 * 1024 * 1024


def _sigmoid(z):
    return 1.0 / (1.0 + jnp.exp(-z))


def _layer_norm(z, g, b):
    mu = jnp.mean(z, axis=-1, keepdims=True)
    zc = z - mu
    var = jnp.mean(zc * zc, axis=-1, keepdims=True)
    return zc * lax.rsqrt(var + LN_EPS) * g + b


def _params(n_axes=1):
    return pltpu.CompilerParams(dimension_semantics=("arbitrary",) * n_axes,
                                vmem_limit_bytes=VMEM_LIMIT)


def _inproj_kernel(x_ref, wq_ref, wk_ref, wv_ref, wu_ref, wg_ref, kb_ref, bg_ref, wmix_ref, pscale_ref,
                   wpbr_ref, qt_ref, ka_ref, vt_ref, ga_ref, gp_ref, ubuf, *, tiles_per_seq, tm, d_model):
    j = pl.program_id(0) % tiles_per_seq
    xb = x_ref[...].astype(BF16)

    def proj(w_ref, lo=None, hi=None):
        w = w_ref[...] if lo is None else w_ref[:, lo:hi]
        return jnp.dot(xb, w, preferred_element_type=F32)

    def proj_t(wt_ref):
        return lax.dot_general(wt_ref[...], xb, (((1,), (1,)), ((), ())), preferred_element_type=F32)

    qt_ref[...] = (proj_t(wq_ref) * (DIFF_HEAD_DIM ** -0.5 * LOG2E)).astype(BF16)
    ka_ref[...] = (proj(wk_ref) + kb_ref[...].astype(F32)).astype(BF16)
    vt_ref[...] = proj_t(wv_ref).astype(BF16)
    u = proj(wu_ref)

    @pl.when(j == 0)
    def _():
        ubuf[0:POOL_HALO, :] = jnp.zeros((POOL_HALO, POOL_WIDTH), F32)

    ubuf[POOL_HALO:, :] = u
    pos = j * tm + lax.broadcasted_iota(I32, (tm, 1), 0)
    parts = []
    for gi, w in enumerate(POOL_WINDOWS):
        lo, hi = gi * POOL_GROUP_DIM, (gi + 1) * POOL_GROUP_DIM
        a = ubuf[:, lo:hi]
        s = a
        sh = 1
        while sh < w:
            s = s + pltpu.roll(s, sh, axis=0)
            sh *= 2
        cnt = jnp.minimum(pos + 1, w).astype(F32)
        pooled = s[POOL_HALO:, :] / cnt - a[POOL_HALO:, :]
        mixed = jnp.dot(pooled.astype(BF16), wmix_ref[gi], preferred_element_type=F32)
        parts.append(mixed * pscale_ref[:, lo:hi])
    ubuf[0:POOL_HALO, :] = u[tm - POOL_HALO:, :]
    p_branch = jnp.dot(jnp.concatenate(parts, axis=-1).astype(BF16), wpbr_ref[...],
                       preferred_element_type=F32)

    g_a = _sigmoid(proj(wg_ref, 0, d_model) + bg_ref[:, 0:d_model])
    ga_ref[...] = g_a.astype(BF16)
    g_p = _sigmoid(proj(wg_ref, d_model, 2 * d_model) + bg_ref[:, d_model:2 * d_model])
    gp_ref[...] = (g_p * p_branch).astype(BF16)


def _inproj(x2, wq_t, wk_aug, wv_t, wu, wg, kbias, b_gate, w_mix, pool_scale, w_pool_br, bsz, seq):
    t, d = x2.shape
    tm = ATTN_TILE
    n_t = seq // tm
    row = lambda i: (i, 0)
    full2 = lambda i: (0, 0)
    blk4 = lambda i: (i // n_t, i % n_t, 0, 0)
    kern = functools.partial(_inproj_kernel, tiles_per_seq=n_t, tm=tm, d_model=d)
    ka_w = wk_aug.shape[1]
    return pl.pallas_call(
        kern,
        grid=(t // tm,),
        in_specs=[
            pl.BlockSpec((tm, d), row),
            pl.BlockSpec(wq_t.shape, full2),
            pl.BlockSpec(wk_aug.shape, full2),
            pl.BlockSpec(wv_t.shape, full2),
            pl.BlockSpec(wu.shape, full2),
            pl.BlockSpec(wg.shape, full2),
            pl.BlockSpec((tm, ka_w), lambda i: (i % n_t, 0)),
            pl.BlockSpec((1, 2 * d), full2),
            pl.BlockSpec(w_mix.shape, lambda i: (0, 0, 0)),
            pl.BlockSpec((1, POOL_WIDTH), full2),
            pl.BlockSpec((POOL_WIDTH, d), full2),
        ],
        out_specs=[
            pl.BlockSpec((None, None, ATTN_WIDTH, tm), blk4),
            pl.BlockSpec((None, None, tm, ka_w), blk4),
            pl.BlockSpec((None, None, ATTN_WIDTH, tm), blk4),
            pl.BlockSpec((tm, d), row),
            pl.BlockSpec((tm, d), row),
        ],
        out_shape=[
            jax.ShapeDtypeStruct((bsz, n_t, ATTN_WIDTH, tm), BF16),
            jax.ShapeDtypeStruct((bsz, n_t, tm, ka_w), BF16),
            jax.ShapeDtypeStruct((bsz, n_t, ATTN_WIDTH, tm), BF16),
            jax.ShapeDtypeStruct((t, d), BF16),
            jax.ShapeDtypeStruct((t, d), BF16),
        ],
        scratch_shapes=[pltpu.VMEM((POOL_HALO + tm, POOL_WIDTH), F32)],
        compiler_params=_params(),
        name="inproj",
    )(x2, wq_t, wk_aug, wv_t, wu, wg, kbias, b_gate, w_mix, pool_scale, w_pool_br)


def _attn_kernel(lvec_ref, sg_ref, qt_ref, ka_ref, vt_ref, o_ref, m_scr, l_scr, acc_scr,
                 *, tq, lam_init):
    qi = pl.program_id(2)
    hd = DIFF_HEAD_DIM
    qt = qt_ref[...]
    sub = lax.broadcasted_iota(I32, (hd, tq), 0)
    ones_rows = jnp.where(sub < N_BIAS_PIECES, 1.0, 0.0).astype(BF16)
    qa = (jnp.concatenate([qt[0:hd], ones_rows], axis=0), jnp.concatenate([qt[hd:2 * hd], ones_rows], axis=0))

    m_scr[...] = jnp.full(m_scr.shape, -jnp.inf, F32)
    l_scr[...] = jnp.zeros(l_scr.shape, F32)
    acc_scr[...] = jnp.zeros(acc_scr.shape, F32)

    def step(ki, masked):
        vt = vt_ref[ki]
        for m in range(2):
            ka = ka_ref[ki, :, m * HEAD_WIDTH:(m + 1) * HEAD_WIDTH]
            for c0 in range(0, tq, QUERY_CHUNK):
                cs = slice(c0, c0 + QUERY_CHUNK)
                z = jnp.dot(ka, qa[m][:, cs], preferred_element_type=F32)
                if masked:
                    r = lax.broadcasted_iota(I32, z.shape, 0)
                    c = lax.broadcasted_iota(I32, z.shape, 1) + c0
                    z = jnp.where(r <= c, z, -jnp.inf)
                m_old = m_scr[m, :, cs]
                m_new = jnp.maximum(m_old, jnp.max(z, axis=0, keepdims=True))
                alpha = jnp.exp2(m_old - m_new)
                p = jnp.exp2(z - m_new)
                l_scr[m, :, cs] = alpha * l_scr[m, :, cs] + jnp.sum(p, axis=0, keepdims=True)
                acc_scr[m, :, cs] = alpha * acc_scr[m, :, cs] + jnp.dot(vt, p.astype(BF16),
                                                                      preferred_element_type=F32)
                m_scr[m, :, cs] = m_new

    def body(ki, carry):
        step(ki, False)
        return carry

    lax.fori_loop(0, qi, body, 0)
    step(qi, True)

    lv = lvec_ref[...]
    lam = (jnp.exp(jnp.sum(lv[0:1] * lv[1:2], axis=-1, keepdims=True))
           - jnp.exp(jnp.sum(lv[2:3] * lv[3:4], axis=-1, keepdims=True)) + lam_init)
    o = acc_scr[0] / l_scr[0] - lam * (acc_scr[1] / l_scr[1])
    ms = jnp.mean(o * o, axis=0, keepdims=True)
    o = o * lax.rsqrt(ms + LN_EPS) * sg_ref[...] * (1.0 - lam_init)
    o_ref[...] = o.T.astype(BF16)


def _attention(qt, kaug, vt, lvec, subln_g_col, lam_init):
    b, n_t, _, tq = qt.shape
    kern = functools.partial(_attn_kernel, tq=tq, lam_init=lam_init)
    return pl.pallas_call(
        kern,
        grid=(b, N_DIFF_HEADS, n_t),
        in_specs=[
            pl.BlockSpec(lvec.shape, lambda bi, h, i: (0, 0)),
            pl.BlockSpec((HEAD_WIDTH, 1), lambda bi, h, i: (0, 0)),
            pl.BlockSpec((None, None, HEAD_WIDTH, tq), lambda bi, h, i: (bi, i, h, 0)),
            pl.BlockSpec((None, n_t, tq, 2 * HEAD_WIDTH), lambda bi, h, i: (bi, 0, 0, h)),
            pl.BlockSpec((None, n_t, HEAD_WIDTH, tq), lambda bi, h, i: (bi, 0, h, 0)),
        ],
        out_specs=pl.BlockSpec((None, tq, HEAD_WIDTH), lambda bi, h, i: (bi, i, h)),
        out_shape=jax.ShapeDtypeStruct((b, n_t * tq, ATTN_WIDTH), BF16),
        scratch_shapes=[
            pltpu.VMEM((2, 1, tq), F32),
            pltpu.VMEM((2, 1, tq), F32),
            pltpu.VMEM((2, HEAD_WIDTH, tq), F32),
        ],
        compiler_params=_params(3),
        name="diffattn",
    )(lvec, subln_g_col, qt, kaug, vt)


def _mix_kernel(o_ref, ga_ref, gp_ref, x_ref, wabr_ref, wout_ref, g1_ref, b1_ref, wr_ref, br_ref,
                x1_ref, topi_ref, rank_ref, gate_ref, cnt_ref, carry, *, tm, alpha):
    i = pl.program_id(0)

    @pl.when(i == 0)
    def _():
        carry[...] = jnp.zeros(carry.shape, F32)

    a_branch = jnp.dot(o_ref[...], wabr_ref[...], preferred_element_type=F32)
    mixed = ga_ref[...].astype(F32) * a_branch + gp_ref[...].astype(F32)
    y = jnp.dot(mixed.astype(BF16), wout_ref[...], preferred_element_type=F32)
    x1 = _layer_norm(alpha * x_ref[...] + y, g1_ref[...], b1_ref[...])
    x1_ref[...] = x1

    logits = jnp.dot(x1, wr_ref[...], precision=lax.Precision.HIGHEST,
                     preferred_element_type=F32) + br_ref[...]
    e_iota = lax.broadcasted_iota(I32, logits.shape, 1).astype(F32)
    vals, idxs = [], []
    rem = logits
    for _ in range(TOP_K):
        m = jnp.max(rem, axis=-1, keepdims=True)
        idx = jnp.min(jnp.where(rem == m, e_iota, float(N_EXPERTS)), axis=-1, keepdims=True)
        vals.append(m)
        idxs.append(idx)
        rem = jnp.where(e_iota == idx, -jnp.inf, rem)
    exps = [jnp.exp(vk - vals[0]) for vk in vals]
    den = exps[0] + exps[1] + exps[2] + exps[3]

    sel = jnp.zeros(logits.shape, F32)
    for idx in idxs:
        sel = sel + jnp.where(e_iota == idx, 1.0, 0.0)
    r = lax.broadcasted_iota(I32, (tm, tm), 0)
    c = lax.broadcasted_iota(I32, (tm, tm), 1)
    tri = jnp.where(c < r, 1.0, 0.0).astype(BF16)
    prefix = jnp.dot(tri, sel.astype(BF16), preferred_element_type=F32) + carry[...]
    carry[...] = carry[...] + jnp.sum(sel, axis=0, keepdims=True)
    cnt_ref[...] = carry[...]

    lane = lax.broadcasted_iota(I32, (tm, LANES), 1)
    topi = jnp.zeros((tm, LANES), F32)
    rank = jnp.zeros((tm, LANES), F32)
    gate = jnp.zeros((tm, LANES), F32)
    for kk in range(TOP_K):
        rk = jnp.sum(jnp.where(e_iota == idxs[kk], prefix, 0.0), axis=-1, keepdims=True)
        topi = jnp.where(lane == kk, idxs[kk], topi)
        rank = jnp.where(lane == kk, rk, rank)
        gate = jnp.where(lane == kk, exps[kk] / den, gate)
    topi_ref[...] = topi.astype(I32)
    rank_ref[...] = rank.astype(I32)
    gate_ref[...] = gate


def _mix(o2, ga, gp, x2, w_attn_br, w_out, ln_g, ln_b, w_router, b_router, alpha):
    t, d = x2.shape
    tm = TOKEN_TILE
    row = lambda i: (i, 0)
    full2 = lambda i: (0, 0)
    kern = functools.partial(_mix_kernel, tm=tm, alpha=alpha)
    return pl.pallas_call(
        kern,
        grid=(t // tm,),
        in_specs=[
            pl.BlockSpec((tm, ATTN_WIDTH), row),
            pl.BlockSpec((tm, d), row),
            pl.BlockSpec((tm, d), row),
            pl.BlockSpec((tm, d), row),
            pl.BlockSpec((ATTN_WIDTH, d), full2),
            pl.BlockSpec((d, d), full2),
            pl.BlockSpec((1, d), full2),
            pl.BlockSpec((1, d), full2),
            pl.BlockSpec((d, N_EXPERTS), full2),
            pl.BlockSpec((1, N_EXPERTS), full2),
        ],
        out_specs=[
            pl.BlockSpec((tm, d), row),
            pl.BlockSpec((tm, LANES), row),
            pl.BlockSpec((tm, LANES), row),
            pl.BlockSpec((tm, LANES), row),
            pl.BlockSpec((1, N_EXPERTS), full2),
        ],
        out_shape=[
            jax.ShapeDtypeStruct((t, d), F32),
            jax.ShapeDtypeStruct((t, LANES), I32),
            jax.ShapeDtypeStruct((t, LANES), I32),
            jax.ShapeDtypeStruct((t, LANES), F32),
            jax.ShapeDtypeStruct((1, N_EXPERTS), F32),
        ],
        scratch_shapes=[pltpu.VMEM((1, N_EXPERTS), F32)],
        compiler_params=_params(),
        name="mix_ln1_router",
    )(o2, ga, gp, x2, w_attn_br, w_out, ln_g, ln_b, w_router, b_router)


def _row_copy(src_ref, src_row, dst_ref, dst_row, sem):
    return pltpu.make_async_copy(src_ref.at[pl.ds(src_row, 1), :], dst_ref.at[pl.ds(dst_row, 1), :], sem)


def _dispatch_kernel(dest_ref, x1_ref, xs_hbm, sem, *, tm):
    def body(t, carry):
        for kk in range(TOP_K):
            _row_copy(x1_ref, t, xs_hbm, dest_ref[TOP_K * t + kk], sem).start()
        return carry

    lax.fori_loop(0, tm, body, 0)
    for _ in range(TOP_K):
        pltpu.make_async_copy(x1_ref, xs_hbm.at[pl.ds(0, tm), :], sem).wait()


def _dispatch(dest, x1, n_slots):
    t, d = x1.shape
    tm = TOKEN_TILE
    kern = functools.partial(_dispatch_kernel, tm=tm)
    return pl.pallas_call(
        kern,
        grid=(t // tm,),
        in_specs=[
            pl.BlockSpec((TOP_K * tm,), lambda i: (i,), memory_space=pltpu.SMEM),
            pl.BlockSpec((tm, d), lambda i: (i, 0)),
        ],
        out_specs=pl.BlockSpec(memory_space=pl.ANY),
        out_shape=jax.ShapeDtypeStruct((n_slots, d), F32),
        scratch_shapes=[pltpu.SemaphoreType.DMA(())],
        compiler_params=_params(),
        name="dispatch",
    )(dest, x1)


def _expert_kernel(be_ref, bsrc_ref, bv_ref, xs_ref, wup_ref, bup_ref, wdn_ref, bdn_ref, ys_ref,
                   *, bm, d_ff):
    nv = bv_ref[pl.program_id(0)]

    @pl.when(nv > 0)
    def _():
        rows = lax.broadcasted_iota(I32, (bm, 1), 0)
        x = jnp.where(rows < nv, xs_ref[...], 0.0).astype(BF16)
        hid = jnp.dot(x, wup_ref[0], preferred_element_type=F32) + bup_ref[0]
        glu = jnp.minimum(hid[:, :d_ff], SWIGLU_LIMIT)
        lin = jnp.clip(hid[:, d_ff:], -SWIGLU_LIMIT, SWIGLU_LIMIT)
        act = glu * _sigmoid(SWIGLU_ALPHA * glu) * (lin + 1.0)
        ys_ref[...] = jnp.dot(act.astype(BF16), wdn_ref[0], preferred_element_type=F32) + bdn_ref[0]


def _experts(be, bsrc, bv, xs, w_up, b_up, w_down, b_down):
    n_slots, d = xs.shape
    bm = EXPERT_BLOCK
    d_ff = w_down.shape[1]
    kern = functools.partial(_expert_kernel, bm=bm, d_ff=d_ff)
    blk = lambda b, be_r, bsrc_r, bv_r: (bsrc_r[b], 0)
    wsel = lambda b, be_r, bsrc_r, bv_r: (be_r[b], 0, 0)
    grid_spec = pltpu.PrefetchScalarGridSpec(
        num_scalar_prefetch=3,
        grid=(n_slots // bm,),
        in_specs=[
            pl.BlockSpec((bm, d), blk),
            pl.BlockSpec((1, d, 2 * d_ff), wsel),
            pl.BlockSpec((1, 1, 2 * d_ff), wsel),
            pl.BlockSpec((1, d_ff, d), wsel),
            pl.BlockSpec((1, 1, d), wsel),
        ],
        out_specs=pl.BlockSpec((bm, d), blk),
    )
    return pl.pallas_call(
        kern,
        grid_spec=grid_spec,
        out_shape=jax.ShapeDtypeStruct((n_slots, d), F32),
        compiler_params=_params(),
        name="experts",
    )(be, bsrc, bv, xs, w_up, b_up, w_down, b_down)


def _final_kernel(dest_ref, x1_ref, p_ref, gate_ref, wg_ref, wp_ref, g2_ref, b2_ref, ys_hbm,
                  o_ref, ybuf, sem, *, tm, alpha):
    def body(t, carry):
        for kk in range(TOP_K):
            _row_copy(ys_hbm, dest_ref[TOP_K * t + kk], ybuf.at[kk], t, sem).start()
        return carry

    lax.fori_loop(0, tm, body, 0)

    x1 = x1_ref[...]
    ple = (_sigmoid(jnp.dot(x1.astype(BF16), wg_ref[...], preferred_element_type=F32))
           * jnp.dot(p_ref[...].astype(BF16), wp_ref[...], preferred_element_type=F32))

    for kk in range(TOP_K):
        pltpu.make_async_copy(ys_hbm.at[pl.ds(0, tm), :], ybuf.at[kk], sem).wait()
    gate = gate_ref[...]
    ffn = gate[:, 0:1] * ybuf[0]
    for kk in range(1, TOP_K):
        ffn = ffn + gate[:, kk:kk + 1] * ybuf[kk]
    o_ref[...] = _layer_norm(alpha * x1 + ffn + ple, g2_ref[...], b2_ref[...])


def _final(dest, x1, p2, gate, w_ple_gate, w_ple_proj, ln_g, ln_b, ys, alpha):
    t, d = x1.shape
    tm = TOKEN_TILE
    ple_dim = p2.shape[1]
    row = lambda i: (i, 0)
    full2 = lambda i: (0, 0)
    kern = functools.partial(_final_kernel, tm=tm, alpha=alpha)
    return pl.pallas_call(
        kern,
        grid=(t // tm,),
        in_specs=[
            pl.BlockSpec((TOP_K * tm,), lambda i: (i,), memory_space=pltpu.SMEM),
            pl.BlockSpec((tm, d), row),
            pl.BlockSpec((tm, ple_dim), row),
            pl.BlockSpec((tm, LANES), row),
            pl.BlockSpec((d, d), full2),
            pl.BlockSpec((ple_dim, d), full2),
            pl.BlockSpec((1, d), full2),
            pl.BlockSpec((1, d), full2),
            pl.BlockSpec(memory_space=pl.ANY),
        ],
        out_specs=pl.BlockSpec((tm, d), row),
        out_shape=jax.ShapeDtypeStruct((t, d), F32),
        scratch_shapes=[pltpu.VMEM((TOP_K, tm, d), F32), pltpu.SemaphoreType.DMA(())],
        compiler_params=_params(),
        name="combine_ple_ln2",
    )(dest, x1, p2, gate, w_ple_gate, w_ple_proj, ln_g, ln_b, ys)


def _augment_k_weight(wk):
    d = wk.shape[0]
    w = wk.reshape(d, N_DIFF_HEADS * 2, DIFF_HEAD_DIM)
    w = jnp.pad(w, ((0, 0), (0, 0), (0, HEAD_WIDTH - DIFF_HEAD_DIM)))
    return w.reshape(d, N_DIFF_HEADS * 2 * HEAD_WIDTH)


def _alibi_key_bias(seq):
    slopes = np.asarray([2.0 ** (-8.0 * (h + 1) / N_DIFF_HEADS) for h in range(N_DIFF_HEADS)], np.float32)
    b = slopes[None, :] * np.arange(seq, dtype=np.float32)[:, None] * np.float32(LOG2E)
    pieces = []
    rem = b
    for _ in range(N_BIAS_PIECES):
        piece = (rem.view(np.uint32) & np.uint32(0xFFFF0000)).view(np.float32)
        pieces.append(piece)
        rem = rem - piece
    pb = jnp.asarray(np.stack(pieces, axis=-1)).astype(BF16)
    pb = jnp.broadcast_to(pb[:, :, None, :], (seq, N_DIFF_HEADS, 2, N_BIAS_PIECES))
    pb = jnp.pad(pb, ((0, 0), (0, 0), (0, 0), (DIFF_HEAD_DIM, HEAD_WIDTH - DIFF_HEAD_DIM - N_BIAS_PIECES)))
    return pb.reshape(seq, N_DIFF_HEADS * 2 * HEAD_WIDTH)


def _routing_tables(counts, topi, rank, n_blocks):
    bm = EXPERT_BLOCK
    n_blk_e = (counts + bm - 1) // bm
    cum_blk = jnp.cumsum(n_blk_e)
    blk_start = cum_blk - n_blk_e
    n_used = cum_blk[-1]
    experts = jnp.arange(N_EXPERTS, dtype=I32)
    slot_start = jnp.sum(jnp.where(topi[..., None] == experts, blk_start * bm, 0), axis=-1)
    dest = (slot_start + rank).reshape(-1).astype(I32)
    bidx = jnp.arange(n_blocks, dtype=I32)
    bsrc = jnp.minimum(bidx, n_used - 1).astype(I32)
    be = jnp.minimum(jnp.sum((cum_blk[None, :] <= bsrc[:, None]).astype(I32), axis=-1), N_EXPERTS - 1)
    valid = jnp.clip(counts[be] - (bsrc - blk_start[be]) * bm, 0, bm)
    bv = jnp.where(bidx < n_used, valid, 0).astype(I32)
    return dest, be, bsrc, bv


def kernel(x, p, w_in, b_gate, lambda_q1, lambda_k1, lambda_q2, lambda_k2, subln_g, w_attn_br,
           w_pool_mix, pool_scale, w_pool_br, w_out, ln1_g, ln1_b, w_router, b_router, w_up, b_up,
           w_down, b_down, w_ple_gate, w_ple_proj, ln2_g, ln2_b):
    bsz, seq, d = x.shape
    depth = w_in.shape[0]
    t = bsz * seq
    assert seq % ATTN_TILE == 0 and seq % TOKEN_TILE == 0 and t % EXPERT_BLOCK == 0
    alpha = (2 * depth) ** 0.25
    n_blocks = (t * TOP_K) // EXPERT_BLOCK + N_EXPERTS
    n_slots = n_blocks * EXPERT_BLOCK

    x2 = x.reshape(t, d)
    for i in range(depth):
        lam_init = 0.8 - 0.6 * math.exp(-0.3 * i)
        wq, wk, wv, wu, wg = jnp.split(w_in[i].astype(BF16), IN_SPLITS, axis=-1)
        qt, kaug, vt, ga, gp = _inproj(x2, wq.T, _augment_k_weight(wk), wv.T, wu, wg, _alibi_key_bias(seq),
                                       b_gate[i][None], w_pool_mix[i].astype(BF16), pool_scale[i][None],
                                       w_pool_br[i].astype(BF16), bsz, seq)
        lvec = jnp.stack([lambda_q1[i], lambda_k1[i], lambda_q2[i], lambda_k2[i]]).astype(F32)
        o = _attention(qt, kaug, vt, lvec, subln_g[i][:, None], lam_init)
        x1, topi, rank, gate, cnt = _mix(o.reshape(t, ATTN_WIDTH), ga, gp, x2,
                                         w_attn_br[i].astype(BF16), w_out[i].astype(BF16),
                                         ln1_g[i][None], ln1_b[i][None], w_router[i], b_router[i][None],
                                         alpha)
        counts = cnt[0].astype(I32)
        dest, be, bsrc, bv = _routing_tables(counts, topi[:, :TOP_K], rank[:, :TOP_K], n_blocks)
        xs = _dispatch(dest, x1, n_slots)
        ys = _experts(be, bsrc, bv, xs, w_up[i].astype(BF16), b_up[i][:, None, :],
                      w_down[i].astype(BF16), b_down[i][:, None, :])
        x2 = _final(dest, x1, p[i].reshape(t, -1), gate, w_ple_gate[i].astype(BF16),
                    w_ple_proj[i].astype(BF16), ln2_g[i][None], ln2_b[i][None], ys, alpha)
    return x2.reshape(bsz, seq, d)
```

```python
import functools
import math

import jax
import jax.numpy as jnp
import numpy as np
from jax import lax
from jax.experimental import pallas as pl
from jax.experimental.pallas import tpu as pltpu

F32 = jnp.float32
BF16 = jnp.bfloat16
I32 = jnp.int32

N_DIFF_HEADS = 4
DIFF_HEAD_DIM = 64
HEAD_WIDTH = 2 * DIFF_HEAD_DIM
ATTN_WIDTH = N_DIFF_HEADS * HEAD_WIDTH
POOL_WINDOWS = (2, 4, 8, 16)
POOL_GROUP_DIM = 128
POOL_WIDTH = len(POOL_WINDOWS) * POOL_GROUP_DIM
POOL_HALO = 16
N_EXPERTS = 32
TOP_K = 4
SWIGLU_LIMIT = 7.0
SWIGLU_ALPHA = 1.702
LN_EPS = 1e-5
LANES = 128
LOG2E = math.log2(math.e)
N_BIAS_PIECES = 3
IN_SPLITS = (ATTN_WIDTH, 2 * ATTN_WIDTH, 3 * ATTN_WIDTH, 3 * ATTN_WIDTH + POOL_WIDTH)

TOKEN_TILE = 512
ATTN_TILE = 512
SUM_ROWS = 16
EXPERT_BLOCK = 512
COMBINE_TILE = 512
ISSUE_UNROLL = 4
VMEM_LIMIT = 56 * 1024 * 1024


def _sigmoid(z):
    return 1.0 / (1.0 + jnp.exp(-z))


def _layer_norm(z, g, b):
    mu = jnp.mean(z, axis=-1, keepdims=True)
    zc = z - mu
    var = jnp.mean(zc * zc, axis=-1, keepdims=True)
    return zc * lax.rsqrt(var + LN_EPS) * g + b


def _params(n_axes=1):
    return pltpu.CompilerParams(dimension_semantics=("arbitrary",) * n_axes,
                                vmem_limit_bytes=VMEM_LIMIT)


def _inproj_kernel(x_ref, wq_ref, wk_ref, wv_ref, wu_ref, wg_ref, kb_ref, bg_ref, wmix_ref, pscale_ref,
                   wpbr_ref, qt_ref, ka_ref, vt_ref, ga_ref, gp_ref, ubuf, *, tiles_per_seq, tm, d_model):
    j = pl.program_id(0) % tiles_per_seq
    xb = x_ref[...].astype(BF16)

    def proj(w_ref, lo=None, hi=None):
        w = w_ref[...] if lo is None else w_ref[:, lo:hi]
        return jnp.dot(xb, w, preferred_element_type=F32)

    def proj_t(wt_ref):
        return lax.dot_general(wt_ref[...], xb, (((1,), (1,)), ((), ())), preferred_element_type=F32)

    qt_ref[...] = (proj_t(wq_ref) * (DIFF_HEAD_DIM ** -0.5 * LOG2E)).astype(BF16)
    ka_ref[...] = (proj(wk_ref) + kb_ref[...].astype(F32)).astype(BF16)
    vt_ref[...] = proj_t(wv_ref).astype(BF16)
    u = proj(wu_ref)

    @pl.when(j == 0)
    def _():
        ubuf[0:POOL_HALO, :] = jnp.zeros((POOL_HALO, POOL_WIDTH), F32)

    ubuf[POOL_HALO:, :] = u
    pos = j * tm + lax.broadcasted_iota(I32, (tm, 1), 0)
    parts = []
    for gi, w in enumerate(POOL_WINDOWS):
        lo, hi = gi * POOL_GROUP_DIM, (gi + 1) * POOL_GROUP_DIM
        a = ubuf[:, lo:hi]
        s = a
        sh = 1
        while sh < w:
            s = s + pltpu.roll(s, sh, axis=0)
            sh *= 2
        cnt = jnp.minimum(pos + 1, w).astype(F32)
        pooled = s[POOL_HALO:, :] / cnt - a[POOL_HALO:, :]
        mixed = jnp.dot(pooled.astype(BF16), wmix_ref[gi], preferred_element_type=F32)
        parts.append(mixed * pscale_ref[:, lo:hi])
    ubuf[0:POOL_HALO, :] = u[tm - POOL_HALO:, :]
    p_branch = jnp.dot(jnp.concatenate(parts, axis=-1).astype(BF16), wpbr_ref[...],
                       preferred_element_type=F32)

    g_a = _sigmoid(proj(wg_ref, 0, d_model) + bg_ref[:, 0:d_model])
    ga_ref[...] = g_a.astype(BF16)
    g_p = _sigmoid(proj(wg_ref, d_model, 2 * d_model) + bg_ref[:, d_model:2 * d_model])
    gp_ref[...] = (g_p * p_branch).astype(BF16)


def _inproj(x2, wq_t, wk_aug, wv_t, wu, wg, kbias, b_gate, w_mix, pool_scale, w_pool_br, bsz, seq):
    t, d = x2.shape
    tm = ATTN_TILE
    n_t = seq // tm
    row = lambda i: (i, 0)
    full2 = lambda i: (0, 0)
    blk4 = lambda i: (i // n_t, i % n_t, 0, 0)
    kern = functools.partial(_inproj_kernel, tiles_per_seq=n_t, tm=tm, d_model=d)
    ka_w = wk_aug.shape[1]
    return pl.pallas_call(
        kern,
        grid=(t // tm,),
        in_specs=[
            pl.BlockSpec((tm, d), row),
            pl.BlockSpec(wq_t.shape, full2),
            pl.BlockSpec(wk_aug.shape, full2),
            pl.BlockSpec(wv_t.shape, full2),
            pl.BlockSpec(wu.shape, full2),
            pl.BlockSpec(wg.shape, full2),
            pl.BlockSpec((tm, ka_w), lambda i: (i % n_t, 0)),
            pl.BlockSpec((1, 2 * d), full2),
            pl.BlockSpec(w_mix.shape, lambda i: (0, 0, 0)),
            pl.BlockSpec((1, POOL_WIDTH), full2),
            pl.BlockSpec((POOL_WIDTH, d), full2),
        ],
        out_specs=[
            pl.BlockSpec((None, None, ATTN_WIDTH, tm), blk4),
            pl.BlockSpec((None, None, tm, ka_w), blk4),
            pl.BlockSpec((None, None, ATTN_WIDTH, tm), blk4),
            pl.BlockSpec((tm, d), row),
            pl.BlockSpec((tm, d), row),
        ],
        out_shape=[
            jax.ShapeDtypeStruct((bsz, n_t, ATTN_WIDTH, tm), BF16),
            jax.ShapeDtypeStruct((bsz, n_t, tm, ka_w), BF16),
            jax.ShapeDtypeStruct((bsz, n_t, ATTN_WIDTH, tm), BF16),
            jax.ShapeDtypeStruct((t, d), BF16),
            jax.ShapeDtypeStruct((t, d), BF16),
        ],
        scratch_shapes=[pltpu.VMEM((POOL_HALO + tm, POOL_WIDTH), F32)],
        compiler_params=_params(),
        name="inproj",
    )(x2, wq_t, wk_aug, wv_t, wu, wg, kbias, b_gate, w_mix, pool_scale, w_pool_br)


def _attn_kernel(lvec_ref, sg_ref, qt_ref, ka_ref, vt_ref, o_ref, m_scr, acc_scr, *, tq, lam_init):
    qi = pl.program_id(2)
    hd = DIFF_HEAD_DIM
    qt = qt_ref[...]
    sub = lax.broadcasted_iota(I32, (hd, tq), 0)
    ones_rows = jnp.where(sub < N_BIAS_PIECES, 1.0, 0.0).astype(BF16)
    qa = (jnp.concatenate([qt[0:hd], ones_rows], axis=0), jnp.concatenate([qt[hd:2 * hd], ones_rows], axis=0))

    m_scr[...] = jnp.full(m_scr.shape, -jnp.inf, F32)
    acc_scr[...] = jnp.zeros(acc_scr.shape, F32)
    sum_rows = jnp.ones((SUM_ROWS, tq), BF16)

    def step(ki, masked):
        vt = jnp.concatenate([vt_ref[ki], sum_rows], axis=0)
        zs = []
        for m in range(2):
            ka = ka_ref[ki, :, m * HEAD_WIDTH:(m + 1) * HEAD_WIDTH]
            z = jnp.dot(ka, qa[m], preferred_element_type=F32)
            if masked:
                r = lax.broadcasted_iota(I32, z.shape, 0)
                c = lax.broadcasted_iota(I32, z.shape, 1)
                z = jnp.where(r <= c, z, -jnp.inf)
            zs.append(z)
        for m in range(2):
            z = zs[m]
            m_old = m_scr[m]
            m_new = jnp.maximum(m_old, jnp.max(z, axis=0, keepdims=True))
            alpha = jnp.exp2(m_old - m_new)
            p = jnp.exp2(z - m_new).astype(BF16)
            acc_scr[m] = alpha * acc_scr[m] + jnp.dot(vt, p, preferred_element_type=F32)
            m_scr[m] = m_new

    def body(ki, carry):
        step(ki, False)
        return carry

    lax.fori_loop(0, qi, body, 0)
    step(qi, True)

    lv = lvec_ref[...]
    lam = (jnp.exp(jnp.sum(lv[0:1] * lv[1:2], axis=-1, keepdims=True))
           - jnp.exp(jnp.sum(lv[2:3] * lv[3:4], axis=-1, keepdims=True)) + lam_init)
    hw = HEAD_WIDTH
    o = (acc_scr[0, 0:hw] / acc_scr[0, hw:hw + 1]
         - lam * (acc_scr[1, 0:hw] / acc_scr[1, hw:hw + 1]))
    ms = jnp.mean(o * o, axis=0, keepdims=True)
    o = o * lax.rsqrt(ms + LN_EPS) * sg_ref[...] * (1.0 - lam_init)
    o_ref[...] = o.T.astype(BF16)


def _attention(qt, kaug, vt, lvec, subln_g_col, lam_init):
    b, n_t, _, tq = qt.shape
    kern = functools.partial(_attn_kernel, tq=tq, lam_init=lam_init)
    return pl.pallas_call(
        kern,
        grid=(b, N_DIFF_HEADS, n_t),
        in_specs=[
            pl.BlockSpec(lvec.shape, lambda bi, h, i: (0, 0)),
            pl.BlockSpec((HEAD_WIDTH, 1), lambda bi, h, i: (0, 0)),
            pl.BlockSpec((None, None, HEAD_WIDTH, tq), lambda bi, h, i: (bi, i, h, 0)),
            pl.BlockSpec((None, n_t, tq, 2 * HEAD_WIDTH), lambda bi, h, i: (bi, 0, 0, h)),
            pl.BlockSpec((None, n_t, HEAD_WIDTH, tq), lambda bi, h, i: (bi, 0, h, 0)),
        ],
        out_specs=pl.BlockSpec((None, tq, HEAD_WIDTH), lambda bi, h, i: (bi, i, h)),
        out_shape=jax.ShapeDtypeStruct((b, n_t * tq, ATTN_WIDTH), BF16),
        scratch_shapes=[
            pltpu.VMEM((2, 1, tq), F32),
            pltpu.VMEM((2, HEAD_WIDTH + SUM_ROWS, tq), F32),
        ],
        compiler_params=_params(3),
        name="diffattn",
    )(lvec, subln_g_col, qt, kaug, vt)


def _mix_kernel(o_ref, ga_ref, gp_ref, x_ref, wabr_ref, wout_ref, g1_ref, b1_ref, wrh_ref, wrl_ref, br_ref,
                x1_ref, topi_ref, rank_ref, gate_ref, cnt_ref, carry, *, tm, alpha):
    i = pl.program_id(0)

    @pl.when(i == 0)
    def _():
        carry[...] = jnp.zeros(carry.shape, F32)

    a_branch = jnp.dot(o_ref[...], wabr_ref[...], preferred_element_type=F32)
    mixed = ga_ref[...].astype(F32) * a_branch + gp_ref[...].astype(F32)
    y = jnp.dot(mixed.astype(BF16), wout_ref[...], preferred_element_type=F32)
    x1 = _layer_norm(alpha * x_ref[...] + y, g1_ref[...], b1_ref[...])
    x1_ref[...] = x1

    x_hi = x1.astype(BF16)
    x_lo = (x1 - x_hi.astype(F32)).astype(BF16)
    logits = (jnp.dot(x_hi, wrh_ref[...], preferred_element_type=F32)
              + jnp.dot(x_lo, wrh_ref[...], preferred_element_type=F32)
              + jnp.dot(x_hi, wrl_ref[...], preferred_element_type=F32)) + br_ref[...]
    e_iota = lax.broadcasted_iota(I32, logits.shape, 1).astype(F32)
    vals, idxs = [], []
    rem = logits
    for _ in range(TOP_K):
        m = jnp.max(rem, axis=-1, keepdims=True)
        idx = jnp.min(jnp.where(rem == m, e_iota, float(N_EXPERTS)), axis=-1, keepdims=True)
        vals.append(m)
        idxs.append(idx)
        rem = jnp.where(e_iota == idx, -jnp.inf, rem)
    exps = [jnp.exp(vk - vals[0]) for vk in vals]
    den = exps[0] + exps[1] + exps[2] + exps[3]

    sel = jnp.zeros(logits.shape, F32)
    for idx in idxs:
        sel = sel + jnp.where(e_iota == idx, 1.0, 0.0)
    r = lax.broadcasted_iota(I32, (tm, tm), 0)
    c = lax.broadcasted_iota(I32, (tm, tm), 1)
    tri = jnp.where(c < r, 1.0, 0.0).astype(BF16)
    prefix = jnp.dot(tri, sel.astype(BF16), preferred_element_type=F32) + carry[...]
    carry[...] = carry[...] + jnp.sum(sel, axis=0, keepdims=True)
    cnt_ref[...] = carry[...]

    lane = lax.broadcasted_iota(I32, (tm, LANES), 1)
    topi = jnp.zeros((tm, LANES), F32)
    rank = jnp.zeros((tm, LANES), F32)
    gate = jnp.zeros((tm, LANES), F32)
    for kk in range(TOP_K):
        rk = jnp.sum(jnp.where(e_iota == idxs[kk], prefix, 0.0), axis=-1, keepdims=True)
        topi = jnp.where(lane == kk, idxs[kk], topi)
        rank = jnp.where(lane == kk, rk, rank)
        gate = jnp.where(lane == kk, exps[kk] / den, gate)
    topi_ref[...] = topi.astype(I32)
    rank_ref[...] = rank.astype(I32)
    gate_ref[...] = gate


def _mix(o2, ga, gp, x2, w_attn_br, w_out, ln_g, ln_b, w_router_hi, w_router_lo, b_router, alpha):
    t, d = x2.shape
    tm = TOKEN_TILE
    row = lambda i: (i, 0)
    full2 = lambda i: (0, 0)
    kern = functools.partial(_mix_kernel, tm=tm, alpha=alpha)
    return pl.pallas_call(
        kern,
        grid=(t // tm,),
        in_specs=[
            pl.BlockSpec((tm, ATTN_WIDTH), row),
            pl.BlockSpec((tm, d), row),
            pl.BlockSpec((tm, d), row),
            pl.BlockSpec((tm, d), row),
            pl.BlockSpec((ATTN_WIDTH, d), full2),
            pl.BlockSpec((d, d), full2),
            pl.BlockSpec((1, d), full2),
            pl.BlockSpec((1, d), full2),
            pl.BlockSpec((d, N_EXPERTS), full2),
            pl.BlockSpec((d, N_EXPERTS), full2),
            pl.BlockSpec((1, N_EXPERTS), full2),
        ],
        out_specs=[
            pl.BlockSpec((tm, d), row),
            pl.BlockSpec((tm, LANES), row),
            pl.BlockSpec((tm, LANES), row),
            pl.BlockSpec((tm, LANES), row),
            pl.BlockSpec((1, N_EXPERTS), full2),
        ],
        out_shape=[
            jax.ShapeDtypeStruct((t, d), F32),
            jax.ShapeDtypeStruct((t, LANES), I32),
            jax.ShapeDtypeStruct((t, LANES), I32),
            jax.ShapeDtypeStruct((t, LANES), F32),
            jax.ShapeDtypeStruct((1, N_EXPERTS), F32),
        ],
        scratch_shapes=[pltpu.VMEM((1, N_EXPERTS), F32)],
        compiler_params=_params(),
        name="mix_ln1_router",
    )(o2, ga, gp, x2, w_attn_br, w_out, ln_g, ln_b, w_router_hi, w_router_lo, b_router)


def _row_copy(src_ref, src_row, dst_ref, dst_row, sem):
    return pltpu.make_async_copy(src_ref.at[pl.ds(src_row, 1), :], dst_ref.at[pl.ds(dst_row, 1), :], sem)


def _dispatch_kernel(dest_ref, x1_ref, xs_hbm, sem, *, tm):
    def body(t, carry):
        for kk in range(TOP_K):
            _row_copy(x1_ref, t, xs_hbm, dest_ref[TOP_K * t + kk], sem).start()
        return carry

    lax.fori_loop(0, tm, body, 0, unroll=ISSUE_UNROLL)
    for _ in range(TOP_K):
        pltpu.make_async_copy(x1_ref, xs_hbm.at[pl.ds(0, tm), :], sem).wait()


def _dispatch(dest, x1, n_slots):
    t, d = x1.shape
    tm = TOKEN_TILE
    kern = functools.partial(_dispatch_kernel, tm=tm)
    return pl.pallas_call(
        kern,
        grid=(t // tm,),
        in_specs=[
            pl.BlockSpec((TOP_K * tm,), lambda i: (i,), memory_space=pltpu.SMEM),
            pl.BlockSpec((tm, d), lambda i: (i, 0)),
        ],
        out_specs=pl.BlockSpec(memory_space=pl.ANY),
        out_shape=jax.ShapeDtypeStruct((n_slots, d), F32),
        scratch_shapes=[pltpu.SemaphoreType.DMA(())],
        compiler_params=_params(),
        name="dispatch",
    )(dest, x1)


def _expert_kernel(be_ref, bsrc_ref, bv_ref, xs_ref, wup_ref, bup_ref, wdn_ref, bdn_ref, ys_ref,
                   wup_bf, wdn_bf, *, bm, d_ff):
    b = pl.program_id(0)
    nv = bv_ref[b]

    @pl.when(jnp.logical_or(b == 0, be_ref[b] != be_ref[jnp.maximum(b - 1, 0)]))
    def _():
        wup_bf[...] = wup_ref[0].astype(BF16)
        wdn_bf[...] = wdn_ref[0].astype(BF16)

    @pl.when(nv > 0)
    def _():
        rows = lax.broadcasted_iota(I32, (bm, 1), 0)
        x = jnp.where(rows < nv, xs_ref[...], 0.0).astype(BF16)
        hid = jnp.dot(x, wup_bf[...], preferred_element_type=F32) + bup_ref[0]
        glu = jnp.minimum(hid[:, :d_ff], SWIGLU_LIMIT)
        lin = jnp.clip(hid[:, d_ff:], -SWIGLU_LIMIT, SWIGLU_LIMIT)
        act = glu * _sigmoid(SWIGLU_ALPHA * glu) * (lin + 1.0)
        ys_ref[...] = jnp.dot(act.astype(BF16), wdn_bf[...], preferred_element_type=F32) + bdn_ref[0]


def _experts(be, bsrc, bv, xs, w_up, b_up, w_down, b_down):
    n_slots, d = xs.shape
    bm = EXPERT_BLOCK
    d_ff = w_down.shape[1]
    kern = functools.partial(_expert_kernel, bm=bm, d_ff=d_ff)
    blk = lambda b, be_r, bsrc_r, bv_r: (bsrc_r[b], 0)
    wsel = lambda b, be_r, bsrc_r, bv_r: (be_r[b], 0, 0)
    grid_spec = pltpu.PrefetchScalarGridSpec(
        num_scalar_prefetch=3,
        grid=(n_slots // bm,),
        in_specs=[
            pl.BlockSpec((bm, d), blk),
            pl.BlockSpec((1, d, 2 * d_ff), wsel),
            pl.BlockSpec((1, 1, 2 * d_ff), wsel),
            pl.BlockSpec((1, d_ff, d), wsel),
            pl.BlockSpec((1, 1, d), wsel),
        ],
        out_specs=pl.BlockSpec((bm, d), blk),
        scratch_shapes=[pltpu.VMEM((d, 2 * d_ff), BF16), pltpu.VMEM((d_ff, d), BF16)],
    )
    return pl.pallas_call(
        kern,
        grid_spec=grid_spec,
        out_shape=jax.ShapeDtypeStruct((n_slots, d), F32),
        compiler_params=_params(),
        name="experts",
    )(be, bsrc, bv, xs, w_up, b_up, w_down, b_down)


def _final_kernel(dest_ref, dnext_ref, x1_ref, p_ref, gate_ref, wg_ref, wp_ref, g2_ref, b2_ref, ys_hbm,
                  o_ref, ybuf, sem, *, tm, alpha):
    i = pl.program_id(0)
    slot = i % 2

    def gather(ids_ref, buf_slot):
        def body(t, carry):
            for kk in range(TOP_K):
                _row_copy(ys_hbm, ids_ref[TOP_K * t + kk], ybuf.at[buf_slot, kk], t, sem.at[buf_slot]).start()
            return carry

        lax.fori_loop(0, tm, body, 0, unroll=ISSUE_UNROLL)

    @pl.when(i == 0)
    def _():
        gather(dest_ref, 0)

    @pl.when(i + 1 < pl.num_programs(0))
    def _():
        gather(dnext_ref, 1 - slot)

    x1 = x1_ref[...]
    ple = (_sigmoid(jnp.dot(x1.astype(BF16), wg_ref[...], preferred_element_type=F32))
           * jnp.dot(p_ref[...].astype(BF16), wp_ref[...], preferred_element_type=F32))

    for kk in range(TOP_K):
        pltpu.make_async_copy(ys_hbm.at[pl.ds(0, tm), :], ybuf.at[slot, kk], sem.at[slot]).wait()
    gate = gate_ref[...]
    ffn = gate[:, 0:1] * ybuf[slot, 0]
    for kk in range(1, TOP_K):
        ffn = ffn + gate[:, kk:kk + 1] * ybuf[slot, kk]
    o_ref[...] = _layer_norm(alpha * x1 + ffn + ple, g2_ref[...], b2_ref[...])


def _final(dest, x1, p2, gate, w_ple_gate, w_ple_proj, ln_g, ln_b, ys, alpha):
    t, d = x1.shape
    tm = COMBINE_TILE
    n_tiles = t // tm
    ple_dim = p2.shape[1]
    row = lambda i: (i, 0)
    full2 = lambda i: (0, 0)
    kern = functools.partial(_final_kernel, tm=tm, alpha=alpha)
    return pl.pallas_call(
        kern,
        grid=(n_tiles,),
        in_specs=[
            pl.BlockSpec((TOP_K * tm,), lambda i: (i,), memory_space=pltpu.SMEM),
            pl.BlockSpec((TOP_K * tm,), lambda i: (jnp.minimum(i + 1, n_tiles - 1),), memory_space=pltpu.SMEM),
            pl.BlockSpec((tm, d), row),
            pl.BlockSpec((tm, ple_dim), row),
            pl.BlockSpec((tm, LANES), row),
            pl.BlockSpec((d, d), full2),
            pl.BlockSpec((ple_dim, d), full2),
            pl.BlockSpec((1, d), full2),
            pl.BlockSpec((1, d), full2),
            pl.BlockSpec(memory_space=pl.ANY),
        ],
        out_specs=pl.BlockSpec((tm, d), row),
        out_shape=jax.ShapeDtypeStruct((t, d), F32),
        scratch_shapes=[pltpu.VMEM((2, TOP_K, tm, d), F32), pltpu.SemaphoreType.DMA((2,))],
        compiler_params=_params(),
        name="combine_ple_ln2",
    )(dest, dest, x1, p2, gate, w_ple_gate, w_ple_proj, ln_g, ln_b, ys)


def _augment_k_weight(wk):
    d = wk.shape[0]
    w = wk.reshape(d, N_DIFF_HEADS * 2, DIFF_HEAD_DIM)
    w = jnp.pad(w, ((0, 0), (0, 0), (0, HEAD_WIDTH - DIFF_HEAD_DIM)))
    return w.reshape(d, N_DIFF_HEADS * 2 * HEAD_WIDTH)


def _alibi_key_bias(seq):
    slopes = np.asarray([2.0 ** (-8.0 * (h + 1) / N_DIFF_HEADS) for h in range(N_DIFF_HEADS)], np.float32)
    b = slopes[None, :] * np.arange(seq, dtype=np.float32)[:, None] * np.float32(LOG2E)
    pieces = []
    rem = b
    for _ in range(N_BIAS_PIECES):
        piece = (rem.view(np.uint32) & np.uint32(0xFFFF0000)).view(np.float32)
        pieces.append(piece)
        rem = rem - piece
    pb = jnp.asarray(np.stack(pieces, axis=-1)).astype(BF16)
    pb = jnp.broadcast_to(pb[:, :, None, :], (seq, N_DIFF_HEADS, 2, N_BIAS_PIECES))
    pb = jnp.pad(pb, ((0, 0), (0, 0), (0, 0), (DIFF_HEAD_DIM, HEAD_WIDTH - DIFF_HEAD_DIM - N_BIAS_PIECES)))
    return pb.reshape(seq, N_DIFF_HEADS * 2 * HEAD_WIDTH)


def _split_bf16(w):
    hi = lax.bitcast_convert_type(lax.bitcast_convert_type(w, jnp.uint32) & jnp.uint32(0xFFFF0000), F32)
    return hi.astype(BF16), (w - hi).astype(BF16)


def _routing_tables(counts, topi, rank, n_blocks):
    bm = EXPERT_BLOCK
    n_blk_e = (counts + bm - 1) // bm
    cum_blk = jnp.cumsum(n_blk_e)
    blk_start = cum_blk - n_blk_e
    n_used = cum_blk[-1]
    experts = jnp.arange(N_EXPERTS, dtype=I32)
    slot_start = jnp.sum(jnp.where(topi[..., None] == experts, blk_start * bm, 0), axis=-1)
    dest = (slot_start + rank).reshape(-1).astype(I32)
    bidx = jnp.arange(n_blocks, dtype=I32)
    bsrc = jnp.minimum(bidx, n_used - 1).astype(I32)
    be = jnp.minimum(jnp.sum((cum_blk[None, :] <= bsrc[:, None]).astype(I32), axis=-1), N_EXPERTS - 1)
    valid = jnp.clip(counts[be] - (bsrc - blk_start[be]) * bm, 0, bm)
    bv = jnp.where(bidx < n_used, valid, 0).astype(I32)
    return dest, be, bsrc, bv


def kernel(x, p, w_in, b_gate, lambda_q1, lambda_k1, lambda_q2, lambda_k2, subln_g, w_attn_br,
           w_pool_mix, pool_scale, w_pool_br, w_out, ln1_g, ln1_b, w_router, b_router, w_up, b_up,
           w_down, b_down, w_ple_gate, w_ple_proj, ln2_g, ln2_b):
    bsz, seq, d = x.shape
    depth = w_in.shape[0]
    t = bsz * seq
    assert seq % ATTN_TILE == 0 and seq % TOKEN_TILE == 0 and t % EXPERT_BLOCK == 0
    alpha = (2 * depth) ** 0.25
    n_blocks = (t * TOP_K) // EXPERT_BLOCK + N_EXPERTS
    n_slots = n_blocks * EXPERT_BLOCK

    x2 = x.reshape(t, d)
    for i in range(depth):
        lam_init = 0.8 - 0.6 * math.exp(-0.3 * i)
        wq, wk, wv, wu, wg = jnp.split(w_in[i].astype(BF16), IN_SPLITS, axis=-1)
        qt, kaug, vt, ga, gp = _inproj(x2, wq.T, _augment_k_weight(wk), wv.T, wu, wg, _alibi_key_bias(seq),
                                       b_gate[i][None], w_pool_mix[i].astype(BF16), pool_scale[i][None],
                                       w_pool_br[i].astype(BF16), bsz, seq)
        lvec = jnp.stack([lambda_q1[i], lambda_k1[i], lambda_q2[i], lambda_k2[i]]).astype(F32)
        o = _attention(qt, kaug, vt, lvec, subln_g[i][:, None], lam_init)
        x1, topi, rank, gate, cnt = _mix(o.reshape(t, ATTN_WIDTH), ga, gp, x2,
                                         w_attn_br[i].astype(BF16), w_out[i].astype(BF16),
                                         ln1_g[i][None], ln1_b[i][None], *_split_bf16(w_router[i]),
                                         b_router[i][None], alpha)
        counts = cnt[0].astype(I32)
        dest, be, bsrc, bv = _routing_tables(counts, topi[:, :TOP_K], rank[:, :TOP_K], n_blocks)
        xs = _dispatch(dest, x1, n_slots)
        ys = _experts(be, bsrc, bv, xs, w_up[i], b_up[i][:, None, :], w_down[i], b_down[i][:, None, :])
        x2 = _final(dest, x1, p[i].reshape(t, -1), gate, w_ple_gate[i].astype(BF16),
                    w_ple_proj[i].astype(BF16), ln2_g[i][None], ln2_b[i][None], ys, alpha)
    return x2.reshape(bsz, seq, d)
```

```python
import functools
import math

import jax
import jax.numpy as jnp
import numpy as np
from jax import lax
from jax.experimental import pallas as pl
from jax.experimental.pallas import tpu as pltpu

F32 = jnp.float32
BF16 = jnp.bfloat16
I32 = jnp.int32

N_DIFF_HEADS = 4
DIFF_HEAD_DIM = 64
HEAD_WIDTH = 2 * DIFF_HEAD_DIM
ATTN_WIDTH = N_DIFF_HEADS * HEAD_WIDTH
POOL_WINDOWS = (2, 4, 8, 16)
POOL_GROUP_DIM = 128
POOL_WIDTH = len(POOL_WINDOWS) * POOL_GROUP_DIM
POOL_HALO = 16
N_EXPERTS = 32
TOP_K = 4
SWIGLU_LIMIT = 7.0
SWIGLU_ALPHA = 1.702
LN_EPS = 1e-5
LANES = 128
LOG2E = math.log2(math.e)
N_BIAS_PIECES = 3
IN_SPLITS = (ATTN_WIDTH, 2 * ATTN_WIDTH, 3 * ATTN_WIDTH, 3 * ATTN_WIDTH + POOL_WIDTH)

TOKEN_TILE = 512
ATTN_TILE = 512
SUM_ROWS = 16
EXPERT_BLOCK = 512
COMBINE_TILE = 512
ISSUE_UNROLL = 4
VMEM_LIMIT = 56 * 1024 * 1024


def _sigmoid(z):
    return 1.0 / (1.0 + jnp.exp(-z))


def _layer_norm(z, g, b):
    mu = jnp.mean(z, axis=-1, keepdims=True)
    zc = z - mu
    var = jnp.mean(zc * zc, axis=-1, keepdims=True)
    return zc * lax.rsqrt(var + LN_EPS) * g + b


def _params(n_axes=1):
    return pltpu.CompilerParams(dimension_semantics=("arbitrary",) * n_axes,
                                vmem_limit_bytes=VMEM_LIMIT)


def _inproj_kernel(x_ref, wq_ref, wk_ref, wv_ref, wu_ref, wg_ref, kb_ref, bg_ref, wmix_ref, pscale_ref,
                   wpbr_ref, qt_ref, ka_ref, vt_ref, ga_ref, gp_ref, ubuf, *, tiles_per_seq, tm, d_model):
    j = pl.program_id(0) % tiles_per_seq
    xb = x_ref[...].astype(BF16)

    def proj(w_ref, lo=None, hi=None):
        w = w_ref[...] if lo is None else w_ref[:, lo:hi]
        return jnp.dot(xb, w, preferred_element_type=F32)

    def proj_t(wt_ref):
        return lax.dot_general(wt_ref[...], xb, (((1,), (1,)), ((), ())), preferred_element_type=F32)

    qt_ref[...] = (proj_t(wq_ref) * (DIFF_HEAD_DIM ** -0.5 * LOG2E)).astype(BF16)
    ka_ref[...] = (proj(wk_ref) + kb_ref[...].astype(F32)).astype(BF16)
    vt_ref[...] = proj_t(wv_ref).astype(BF16)
    u = proj(wu_ref)

    @pl.when(j == 0)
    def _():
        ubuf[0:POOL_HALO, :] = jnp.zeros((POOL_HALO, POOL_WIDTH), F32)

    ubuf[POOL_HALO:, :] = u
    pos = j * tm + lax.broadcasted_iota(I32, (tm, 1), 0)
    parts = []
    for gi, w in enumerate(POOL_WINDOWS):
        lo, hi = gi * POOL_GROUP_DIM, (gi + 1) * POOL_GROUP_DIM
        a = ubuf[:, lo:hi]
        s = a
        sh = 1
        while sh < w:
            s = s + pltpu.roll(s, sh, axis=0)
            sh *= 2
        cnt = jnp.minimum(pos + 1, w).astype(F32)
        pooled = s[POOL_HALO:, :] / cnt - a[POOL_HALO:, :]
        mixed = jnp.dot(pooled.astype(BF16), wmix_ref[gi], preferred_element_type=F32)
        parts.append(mixed * pscale_ref[:, lo:hi])
    ubuf[0:POOL_HALO, :] = u[tm - POOL_HALO:, :]
    p_branch = jnp.dot(jnp.concatenate(parts, axis=-1).astype(BF16), wpbr_ref[...],
                       preferred_element_type=F32)

    g_a = _sigmoid(proj(wg_ref, 0, d_model) + bg_ref[:, 0:d_model])
    ga_ref[...] = g_a.astype(BF16)
    g_p = _sigmoid(proj(wg_ref, d_model, 2 * d_model) + bg_ref[:, d_model:2 * d_model])
    gp_ref[...] = (g_p * p_branch).astype(BF16)


def _inproj(x2, wq_t, wk_aug, wv_t, wu, wg, kbias, b_gate, w_mix, pool_scale, w_pool_br, bsz, seq):
    t, d = x2.shape
    tm = ATTN_TILE
    n_t = seq // tm
    row = lambda i: (i, 0)
    full2 = lambda i: (0, 0)
    blk4 = lambda i: (i // n_t, i % n_t, 0, 0)
    kern = functools.partial(_inproj_kernel, tiles_per_seq=n_t, tm=tm, d_model=d)
    ka_w = wk_aug.shape[1]
    return pl.pallas_call(
        kern,
        grid=(t // tm,),
        in_specs=[
            pl.BlockSpec((tm, d), row),
            pl.BlockSpec(wq_t.shape, full2),
            pl.BlockSpec(wk_aug.shape, full2),
            pl.BlockSpec(wv_t.shape, full2),
            pl.BlockSpec(wu.shape, full2),
            pl.BlockSpec(wg.shape, full2),
            pl.BlockSpec((tm, ka_w), lambda i: (i % n_t, 0)),
            pl.BlockSpec((1, 2 * d), full2),
            pl.BlockSpec(w_mix.shape, lambda i: (0, 0, 0)),
            pl.BlockSpec((1, POOL_WIDTH), full2),
            pl.BlockSpec((POOL_WIDTH, d), full2),
        ],
        out_specs=[
            pl.BlockSpec((None, None, ATTN_WIDTH, tm), blk4),
            pl.BlockSpec((None, None, tm, ka_w), blk4),
            pl.BlockSpec((None, None, ATTN_WIDTH, tm), blk4),
            pl.BlockSpec((tm, d), row),
            pl.BlockSpec((tm, d), row),
        ],
        out_shape=[
            jax.ShapeDtypeStruct((bsz, n_t, ATTN_WIDTH, tm), BF16),
            jax.ShapeDtypeStruct((bsz, n_t, tm, ka_w), BF16),
            jax.ShapeDtypeStruct((bsz, n_t, ATTN_WIDTH, tm), BF16),
            jax.ShapeDtypeStruct((t, d), BF16),
            jax.ShapeDtypeStruct((t, d), BF16),
        ],
        scratch_shapes=[pltpu.VMEM((POOL_HALO + tm, POOL_WIDTH), F32)],
        compiler_params=_params(),
        name="inproj",
    )(x2, wq_t, wk_aug, wv_t, wu, wg, kbias, b_gate, w_mix, pool_scale, w_pool_br)


def _attn_kernel(lvec_ref, sg_ref, qt_ref, ka_ref, vt_ref, o_ref, m_scr, acc_scr, *, tq, lam_init):
    qi = pl.program_id(2)
    hd = DIFF_HEAD_DIM
    qt = qt_ref[...]
    sub = lax.broadcasted_iota(I32, (hd, tq), 0)
    ones_rows = jnp.where(sub < N_BIAS_PIECES, 1.0, 0.0).astype(BF16)
    qa = (jnp.concatenate([qt[0:hd], ones_rows], axis=0), jnp.concatenate([qt[hd:2 * hd], ones_rows], axis=0))

    m_scr[...] = jnp.full(m_scr.shape, -jnp.inf, F32)
    acc_scr[...] = jnp.zeros(acc_scr.shape, F32)
    sum_rows = jnp.ones((SUM_ROWS, tq), BF16)

    def step(ki, masked):
        vt = jnp.concatenate([vt_ref[ki], sum_rows], axis=0)
        zs = []
        for m in range(2):
            ka = ka_ref[ki, :, m * HEAD_WIDTH:(m + 1) * HEAD_WIDTH]
            z = jnp.dot(ka, qa[m], preferred_element_type=F32)
            if masked:
                r = lax.broadcasted_iota(I32, z.shape, 0)
                c = lax.broadcasted_iota(I32, z.shape, 1)
                z = jnp.where(r <= c, z, -jnp.inf)
            zs.append(z)
        for m in range(2):
            z = zs[m]
            m_old = m_scr[m]
            m_new = jnp.maximum(m_old, jnp.max(z, axis=0, keepdims=True))
            alpha = jnp.exp2(m_old - m_new)
            p = jnp.exp2(z - m_new).astype(BF16)
            acc_scr[m] = alpha * acc_scr[m] + jnp.dot(vt, p, preferred_element_type=F32)
            m_scr[m] = m_new

    def body(ki, carry):
        step(ki, False)
        return carry

    lax.fori_loop(0, qi, body, 0)
    step(qi, True)

    lv = lvec_ref[...]
    lam = (jnp.exp(jnp.sum(lv[0:1] * lv[1:2], axis=-1, keepdims=True))
           - jnp.exp(jnp.sum(lv[2:3] * lv[3:4], axis=-1, keepdims=True)) + lam_init)
    hw = HEAD_WIDTH
    o = (acc_scr[0, 0:hw] / acc_scr[0, hw:hw + 1]
         - lam * (acc_scr[1, 0:hw] / acc_scr[1, hw:hw + 1]))
    ms = jnp.mean(o * o, axis=0, keepdims=True)
    o = o * lax.rsqrt(ms + LN_EPS) * sg_ref[...] * (1.0 - lam_init)
    o_ref[...] = o.T.astype(BF16)


def _attention(qt, kaug, vt, lvec, subln_g_col, lam_init):
    b, n_t, _, tq = qt.shape
    kern = functools.partial(_attn_kernel, tq=tq, lam_init=lam_init)
    return pl.pallas_call(
        kern,
        grid=(b, N_DIFF_HEADS, n_t),
        in_specs=[
            pl.BlockSpec(lvec.shape, lambda bi, h, i: (0, 0)),
            pl.BlockSpec((HEAD_WIDTH, 1), lambda bi, h, i: (0, 0)),
            pl.BlockSpec((None, None, HEAD_WIDTH, tq), lambda bi, h, i: (bi, i, h, 0)),
            pl.BlockSpec((None, n_t, tq, 2 * HEAD_WIDTH), lambda bi, h, i: (bi, 0, 0, h)),
            pl.BlockSpec((None, n_t, HEAD_WIDTH, tq), lambda bi, h, i: (bi, 0, h, 0)),
        ],
        out_specs=pl.BlockSpec((None, tq, HEAD_WIDTH), lambda bi, h, i: (bi, i, h)),
        out_shape=jax.ShapeDtypeStruct((b, n_t * tq, ATTN_WIDTH), BF16),
        scratch_shapes=[
            pltpu.VMEM((2, 1, tq), F32),
            pltpu.VMEM((2, HEAD_WIDTH + SUM_ROWS, tq), F32),
        ],
        compiler_params=_params(3),
        name="diffattn",
    )(lvec, subln_g_col, qt, kaug, vt)


def _mix_kernel(o_ref, ga_ref, gp_ref, x_ref, wabr_ref, wout_ref, g1_ref, b1_ref, wrh_ref, wrl_ref, br_ref,
                x1_ref, topi_ref, rank_ref, gate_ref, cnt_ref, carry, *, tm, alpha):
    i = pl.program_id(0)

    @pl.when(i == 0)
    def _():
        carry[...] = jnp.zeros(carry.shape, F32)

    a_branch = jnp.dot(o_ref[...], wabr_ref[...], preferred_element_type=F32)
    mixed = ga_ref[...].astype(F32) * a_branch + gp_ref[...].astype(F32)
    y = jnp.dot(mixed.astype(BF16), wout_ref[...], preferred_element_type=F32)
    x1 = _layer_norm(alpha * x_ref[...] + y, g1_ref[...], b1_ref[...])
    x1_ref[...] = x1

    x_hi = x1.astype(BF16)
    x_lo = (x1 - x_hi.astype(F32)).astype(BF16)
    logits = (jnp.dot(x_hi, wrh_ref[...], preferred_element_type=F32)
              + jnp.dot(x_lo, wrh_ref[...], preferred_element_type=F32)
              + jnp.dot(x_hi, wrl_ref[...], preferred_element_type=F32)) + br_ref[...]
    e_iota = lax.broadcasted_iota(I32, logits.shape, 1).astype(F32)
    vals, idxs = [], []
    rem = logits
    for _ in range(TOP_K):
        m = jnp.max(rem, axis=-1, keepdims=True)
        idx = jnp.min(jnp.where(rem == m, e_iota, float(N_EXPERTS)), axis=-1, keepdims=True)
        vals.append(m)
        idxs.append(idx)
        rem = jnp.where(e_iota == idx, -jnp.inf, rem)
    exps = [jnp.exp(vk - vals[0]) for vk in vals]
    den = exps[0] + exps[1] + exps[2] + exps[3]

    sel = jnp.zeros(logits.shape, F32)
    for idx in idxs:
        sel = sel + jnp.where(e_iota == idx, 1.0, 0.0)
    r = lax.broadcasted_iota(I32, (tm, tm), 0)
    c = lax.broadcasted_iota(I32, (tm, tm), 1)
    tri = jnp.where(c < r, 1.0, 0.0).astype(BF16)
    prefix = jnp.dot(tri, sel.astype(BF16), preferred_element_type=F32) + carry[...]
    carry[...] = carry[...] + jnp.sum(sel, axis=0, keepdims=True)
    cnt_ref[...] = carry[...]

    lane = lax.broadcasted_iota(I32, (tm, LANES), 1)
    topi = jnp.zeros((tm, LANES), F32)
    rank = jnp.zeros((tm, LANES), F32)
    gate = jnp.zeros((tm, LANES), F32)
    for kk in range(TOP_K):
        rk = jnp.sum(jnp.where(e_iota == idxs[kk], prefix, 0.0), axis=-1, keepdims=True)
        topi = jnp.where(lane == kk, idxs[kk], topi)
        rank = jnp.where(lane == kk, rk, rank)
        gate = jnp.where(lane == kk, exps[kk] / den, gate)
    topi_ref[...] = topi.astype(I32)
    rank_ref[...] = rank.astype(I32)
    gate_ref[...] = gate


def _mix(o2, ga, gp, x2, w_attn_br, w_out, ln_g, ln_b, w_router_hi, w_router_lo, b_router, alpha):
    t, d = x2.shape
    tm = TOKEN_TILE
    row = lambda i: (i, 0)
    full2 = lambda i: (0, 0)
    kern = functools.partial(_mix_kernel, tm=tm, alpha=alpha)
    return pl.pallas_call(
        kern,
        grid=(t // tm,),
        in_specs=[
            pl.BlockSpec((tm, ATTN_WIDTH), row),
            pl.BlockSpec((tm, d), row),
            pl.BlockSpec((tm, d), row),
            pl.BlockSpec((tm, d), row),
            pl.BlockSpec((ATTN_WIDTH, d), full2),
            pl.BlockSpec((d, d), full2),
            pl.BlockSpec((1, d), full2),
            pl.BlockSpec((1, d), full2),
            pl.BlockSpec((d, N_EXPERTS), full2),
            pl.BlockSpec((d, N_EXPERTS), full2),
            pl.BlockSpec((1, N_EXPERTS), full2),
        ],
        out_specs=[
            pl.BlockSpec((tm, d), row),
            pl.BlockSpec((tm, LANES), row),
            pl.BlockSpec((tm, LANES), row),
            pl.BlockSpec((tm, LANES), row),
            pl.BlockSpec((1, N_EXPERTS), full2),
        ],
        out_shape=[
            jax.ShapeDtypeStruct((t, d), F32),
            jax.ShapeDtypeStruct((t, LANES), I32),
            jax.ShapeDtypeStruct((t, LANES), I32),
            jax.ShapeDtypeStruct((t, LANES), F32),
            jax.ShapeDtypeStruct((1, N_EXPERTS), F32),
        ],
        scratch_shapes=[pltpu.VMEM((1, N_EXPERTS), F32)],
        compiler_params=_params(),
        name="mix_ln1_router",
    )(o2, ga, gp, x2, w_attn_br, w_out, ln_g, ln_b, w_router_hi, w_router_lo, b_router)


def _row_copy(src_ref, src_row, dst_ref, dst_row, sem):
    return pltpu.make_async_copy(src_ref.at[pl.ds(src_row, 1), :], dst_ref.at[pl.ds(dst_row, 1), :], sem)


def _dispatch_kernel(dest_ref, x1_ref, xs_hbm, sem, *, tm):
    def body(t, carry):
        for kk in range(TOP_K):
            _row_copy(x1_ref, t, xs_hbm, dest_ref[TOP_K * t + kk], sem).start(priority=kk % 2)
        return carry

    lax.fori_loop(0, tm, body, 0, unroll=ISSUE_UNROLL)
    for _ in range(TOP_K):
        pltpu.make_async_copy(x1_ref, xs_hbm.at[pl.ds(0, tm), :], sem).wait()


def _dispatch(dest, x1, n_slots):
    t, d = x1.shape
    tm = TOKEN_TILE
    kern = functools.partial(_dispatch_kernel, tm=tm)
    return pl.pallas_call(
        kern,
        grid=(t // tm,),
        in_specs=[
            pl.BlockSpec((TOP_K * tm,), lambda i: (i,), memory_space=pltpu.SMEM),
            pl.BlockSpec((tm, d), lambda i: (i, 0)),
        ],
        out_specs=pl.BlockSpec(memory_space=pl.ANY),
        out_shape=jax.ShapeDtypeStruct((n_slots, d), F32),
        scratch_shapes=[pltpu.SemaphoreType.DMA(())],
        compiler_params=_params(),
        name="dispatch",
    )(dest, x1)


def _expert_kernel(be_ref, bsrc_ref, bv_ref, xs_ref, wup_ref, bup_ref, wdn_ref, bdn_ref, ys_ref,
                   wup_bf, wdn_bf, *, bm, d_ff):
    b = pl.program_id(0)
    nv = bv_ref[b]

    @pl.when(jnp.logical_or(b == 0, be_ref[b] != be_ref[jnp.maximum(b - 1, 0)]))
    def _():
        wup_bf[...] = wup_ref[0].astype(BF16)
        wdn_bf[...] = wdn_ref[0].astype(BF16)

    @pl.when(nv > 0)
    def _():
        rows = lax.broadcasted_iota(I32, (bm, 1), 0)
        x = jnp.where(rows < nv, xs_ref[...], 0.0).astype(BF16)
        hid = jnp.dot(x, wup_bf[...], preferred_element_type=F32) + bup_ref[0]
        glu = jnp.minimum(hid[:, :d_ff], SWIGLU_LIMIT)
        lin = jnp.clip(hid[:, d_ff:], -SWIGLU_LIMIT, SWIGLU_LIMIT)
        act = glu * _sigmoid(SWIGLU_ALPHA * glu) * (lin + 1.0)
        ys_ref[...] = jnp.dot(act.astype(BF16), wdn_bf[...], preferred_element_type=F32) + bdn_ref[0]


def _experts(be, bsrc, bv, xs, w_up, b_up, w_down, b_down):
    n_slots, d = xs.shape
    bm = EXPERT_BLOCK
    d_ff = w_down.shape[1]
    kern = functools.partial(_expert_kernel, bm=bm, d_ff=d_ff)
    blk = lambda b, be_r, bsrc_r, bv_r: (bsrc_r[b], 0)
    wsel = lambda b, be_r, bsrc_r, bv_r: (be_r[b], 0, 0)
    grid_spec = pltpu.PrefetchScalarGridSpec(
        num_scalar_prefetch=3,
        grid=(n_slots // bm,),
        in_specs=[
            pl.BlockSpec((bm, d), blk),
            pl.BlockSpec((1, d, 2 * d_ff), wsel),
            pl.BlockSpec((1, 1, 2 * d_ff), wsel),
            pl.BlockSpec((1, d_ff, d), wsel),
            pl.BlockSpec((1, 1, d), wsel),
        ],
        out_specs=pl.BlockSpec((bm, d), blk),
        scratch_shapes=[pltpu.VMEM((d, 2 * d_ff), BF16), pltpu.VMEM((d_ff, d), BF16)],
    )
    return pl.pallas_call(
        kern,
        grid_spec=grid_spec,
        out_shape=jax.ShapeDtypeStruct((n_slots, d), F32),
        compiler_params=_params(),
        name="experts",
    )(be, bsrc, bv, xs, w_up, b_up, w_down, b_down)


def _final_kernel(dest_ref, dnext_ref, x1_ref, p_ref, gate_ref, wg_ref, wp_ref, g2_ref, b2_ref, ys_hbm,
                  o_ref, ybuf, sem, *, tm, alpha):
    i = pl.program_id(0)
    slot = i % 2

    def gather(ids_ref, buf_slot):
        def body(t, carry):
            for kk in range(TOP_K):
                _row_copy(ys_hbm, ids_ref[TOP_K * t + kk], ybuf.at[buf_slot, kk], t,
                          sem.at[buf_slot]).start(priority=kk % 2)
            return carry

        lax.fori_loop(0, tm, body, 0, unroll=ISSUE_UNROLL)

    @pl.when(i == 0)
    def _():
        gather(dest_ref, 0)

    @pl.when(i + 1 < pl.num_programs(0))
    def _():
        gather(dnext_ref, 1 - slot)

    x1 = x1_ref[...]
    ple = (_sigmoid(jnp.dot(x1.astype(BF16), wg_ref[...], preferred_element_type=F32))
           * jnp.dot(p_ref[...].astype(BF16), wp_ref[...], preferred_element_type=F32))

    for kk in range(TOP_K):
        pltpu.make_async_copy(ys_hbm.at[pl.ds(0, tm), :], ybuf.at[slot, kk], sem.at[slot]).wait()
    gate = gate_ref[...]
    ffn = gate[:, 0:1] * ybuf[slot, 0]
    for kk in range(1, TOP_K):
        ffn = ffn + gate[:, kk:kk + 1] * ybuf[slot, kk]
    o_ref[...] = _layer_norm(alpha * x1 + ffn + ple, g2_ref[...], b2_ref[...])


def _final(dest, x1, p2, gate, w_ple_gate, w_ple_proj, ln_g, ln_b, ys, alpha):
    t, d = x1.shape
    tm = COMBINE_TILE
    n_tiles = t // tm
    ple_dim = p2.shape[1]
    row = lambda i: (i, 0)
    full2 = lambda i: (0, 0)
    kern = functools.partial(_final_kernel, tm=tm, alpha=alpha)
    return pl.pallas_call(
        kern,
        grid=(n_tiles,),
        in_specs=[
            pl.BlockSpec((TOP_K * tm,), lambda i: (i,), memory_space=pltpu.SMEM),
            pl.BlockSpec((TOP_K * tm,), lambda i: (jnp.minimum(i + 1, n_tiles - 1),), memory_space=pltpu.SMEM),
            pl.BlockSpec((tm, d), row),
            pl.BlockSpec((tm, ple_dim), row),
            pl.BlockSpec((tm, LANES), row),
            pl.BlockSpec((d, d), full2),
            pl.BlockSpec((ple_dim, d), full2),
            pl.BlockSpec((1, d), full2),
            pl.BlockSpec((1, d), full2),
            pl.BlockSpec(memory_space=pl.ANY),
        ],
        out_specs=pl.BlockSpec((tm, d), row),
        out_shape=jax.ShapeDtypeStruct((t, d), F32),
        scratch_shapes=[pltpu.VMEM((2, TOP_K, tm, d), F32), pltpu.SemaphoreType.DMA((2,))],
        compiler_params=_params(),
        name="combine_ple_ln2",
    )(dest, dest, x1, p2, gate, w_ple_gate, w_ple_proj, ln_g, ln_b, ys)


def _augment_k_weight(wk):
    d = wk.shape[0]
    w = wk.reshape(d, N_DIFF_HEADS * 2, DIFF_HEAD_DIM)
    w = jnp.pad(w, ((0, 0), (0, 0), (0, HEAD_WIDTH - DIFF_HEAD_DIM)))
    return w.reshape(d, N_DIFF_HEADS * 2 * HEAD_WIDTH)


def _alibi_key_bias(seq):
    slopes = np.asarray([2.0 ** (-8.0 * (h + 1) / N_DIFF_HEADS) for h in range(N_DIFF_HEADS)], np.float32)
    b = slopes[None, :] * np.arange(seq, dtype=np.float32)[:, None] * np.float32(LOG2E)
    pieces = []
    rem = b
    for _ in range(N_BIAS_PIECES):
        piece = (rem.view(np.uint32) & np.uint32(0xFFFF0000)).view(np.float32)
        pieces.append(piece)
        rem = rem - piece
    pb = jnp.asarray(np.stack(pieces, axis=-1)).astype(BF16)
    pb = jnp.broadcast_to(pb[:, :, None, :], (seq, N_DIFF_HEADS, 2, N_BIAS_PIECES))
    pb = jnp.pad(pb, ((0, 0), (0, 0), (0, 0), (DIFF_HEAD_DIM, HEAD_WIDTH - DIFF_HEAD_DIM - N_BIAS_PIECES)))
    return pb.reshape(seq, N_DIFF_HEADS * 2 * HEAD_WIDTH)


def _split_bf16(w):
    hi = lax.bitcast_convert_type(lax.bitcast_convert_type(w, jnp.uint32) & jnp.uint32(0xFFFF0000), F32)
    return hi.astype(BF16), (w - hi).astype(BF16)


def _routing_tables(counts, topi, rank, n_blocks):
    bm = EXPERT_BLOCK
    n_blk_e = (counts + bm - 1) // bm
    cum_blk = jnp.cumsum(n_blk_e)
    blk_start = cum_blk - n_blk_e
    n_used = cum_blk[-1]
    experts = jnp.arange(N_EXPERTS, dtype=I32)
    slot_start = jnp.sum(jnp.where(topi[..., None] == experts, blk_start * bm, 0), axis=-1)
    dest = (slot_start + rank).reshape(-1).astype(I32)
    bidx = jnp.arange(n_blocks, dtype=I32)
    bsrc = jnp.minimum(bidx, n_used - 1).astype(I32)
    be = jnp.minimum(jnp.sum((cum_blk[None, :] <= bsrc[:, None]).astype(I32), axis=-1), N_EXPERTS - 1)
    valid = jnp.clip(counts[be] - (bsrc - blk_start[be]) * bm, 0, bm)
    bv = jnp.where(bidx < n_used, valid, 0).astype(I32)
    return dest, be, bsrc, bv


def kernel(x, p, w_in, b_gate, lambda_q1, lambda_k1, lambda_q2, lambda_k2, subln_g, w_attn_br,
           w_pool_mix, pool_scale, w_pool_br, w_out, ln1_g, ln1_b, w_router, b_router, w_up, b_up,
           w_down, b_down, w_ple_gate, w_ple_proj, ln2_g, ln2_b):
    bsz, seq, d = x.shape
    depth = w_in.shape[0]
    t = bsz * seq
    assert seq % ATTN_TILE == 0 and seq % TOKEN_TILE == 0 and t % EXPERT_BLOCK == 0
    alpha = (2 * depth) ** 0.25
    n_blocks = (t * TOP_K) // EXPERT_BLOCK + N_EXPERTS
    n_slots = n_blocks * EXPERT_BLOCK

    x2 = x.reshape(t, d)
    for i in range(depth):
        lam_init = 0.8 - 0.6 * math.exp(-0.3 * i)
        wq, wk, wv, wu, wg = jnp.split(w_in[i].astype(BF16), IN_SPLITS, axis=-1)
        qt, kaug, vt, ga, gp = _inproj(x2, wq.T, _augment_k_weight(wk), wv.T, wu, wg, _alibi_key_bias(seq),
                                       b_gate[i][None], w_pool_mix[i].astype(BF16), pool_scale[i][None],
                                       w_pool_br[i].astype(BF16), bsz, seq)
        lvec = jnp.stack([lambda_q1[i], lambda_k1[i], lambda_q2[i], lambda_k2[i]]).astype(F32)
        o = _attention(qt, kaug, vt, lvec, subln_g[i][:, None], lam_init)
        x1, topi, rank, gate, cnt = _mix(o.reshape(t, ATTN_WIDTH), ga, gp, x2,
                                         w_attn_br[i].astype(BF16), w_out[i].astype(BF16),
                                         ln1_g[i][None], ln1_b[i][None], *_split_bf16(w_router[i]),
                                         b_router[i][None], alpha)
        counts = cnt[0].astype(I32)
        dest, be, bsrc, bv = _routing_tables(counts, topi[:, :TOP_K], rank[:, :TOP_K], n_blocks)
        xs = _dispatch(dest, x1, n_slots)
        ys = _experts(be, bsrc, bv, xs, w_up[i], b_up[i][:, None, :], w_down[i], b_down[i][:, None, :])
        x2 = _final(dest, x1, p[i].reshape(t, -1), gate, w_ple_gate[i].astype(BF16),
                    w_ple_proj[i].astype(BF16), ln2_g[i][None], ln2_b[i][None], ys, alpha)
    return x2.reshape(bsz, seq, d)
```

```python
import functools
import math

import jax
import jax.numpy as jnp
import numpy as np
from jax import lax
from jax.experimental import pallas as pl
from jax.experimental.pallas import tpu as pltpu

F32 = jnp.float32
BF16 = jnp.bfloat16
I32 = jnp.int32

N_DIFF_HEADS = 4
DIFF_HEAD_DIM = 64
HEAD_WIDTH = 2 * DIFF_HEAD_DIM
ATTN_WIDTH = N_DIFF_HEADS * HEAD_WIDTH
POOL_WINDOWS = (2, 4, 8, 16)
POOL_GROUP_DIM = 128
POOL_WIDTH = len(POOL_WINDOWS) * POOL_GROUP_DIM
POOL_HALO = 16
N_EXPERTS = 32
TOP_K = 4
SWIGLU_LIMIT = 7.0
SWIGLU_ALPHA = 1.702
LN_EPS = 1e-5
LANES = 128
ROW_TILE = 8
LOG2E = math.log2(math.e)
N_BIAS_PIECES = 3
IN_SPLITS = (ATTN_WIDTH, 2 * ATTN_WIDTH, 3 * ATTN_WIDTH, 3 * ATTN_WIDTH + POOL_WIDTH)

TOKEN_TILE = 512
ATTN_TILE = 512
SUM_ROWS = 16
EXPERT_BLOCK = 512
COMBINE_TILE = 512
ISSUE_UNROLL = 4
VMEM_LIMIT = 56 * 1024 * 1024


def _sigmoid(z):
    return 1.0 / (1.0 + jnp.exp(-z))


def _layer_norm(z, g, b):
    mu = jnp.mean(z, axis=-1, keepdims=True)
    zc = z - mu
    var = jnp.mean(zc * zc, axis=-1, keepdims=True)
    return zc * lax.rsqrt(var + LN_EPS) * g + b


def _params(n_axes=1):
    return pltpu.CompilerParams(dimension_semantics=("arbitrary",) * n_axes,
                                vmem_limit_bytes=VMEM_LIMIT)


def _inproj_kernel(x_ref, wq_ref, wk_ref, wv_ref, wu_ref, wg_ref, kb_ref, bg_ref, wmix_ref, pscale_ref,
                   wpbr_ref, qt_ref, ka_ref, vt_ref, ga_ref, gp_ref, ubuf, *, tiles_per_seq, tm, d_model):
    j = pl.program_id(0) % tiles_per_seq
    xb = x_ref[...].astype(BF16)

    def proj(w_ref, lo=None, hi=None):
        w = w_ref[...] if lo is None else w_ref[:, lo:hi]
        return jnp.dot(xb, w, preferred_element_type=F32)

    def proj_t(wt_ref):
        return lax.dot_general(wt_ref[...], xb, (((1,), (1,)), ((), ())), preferred_element_type=F32)

    qt_ref[...] = (proj_t(wq_ref) * (DIFF_HEAD_DIM ** -0.5 * LOG2E)).astype(BF16)
    ka_ref[...] = (proj(wk_ref) + kb_ref[...].astype(F32)).astype(BF16)
    vt_ref[...] = proj_t(wv_ref).astype(BF16)
    u = proj(wu_ref)

    @pl.when(j == 0)
    def _():
        ubuf[0:POOL_HALO, :] = jnp.zeros((POOL_HALO, POOL_WIDTH), F32)

    ubuf[POOL_HALO:, :] = u
    pos = j * tm + lax.broadcasted_iota(I32, (tm, 1), 0)
    parts = []
    for gi, w in enumerate(POOL_WINDOWS):
        lo, hi = gi * POOL_GROUP_DIM, (gi + 1) * POOL_GROUP_DIM
        a = ubuf[:, lo:hi]
        s = a
        sh = 1
        while sh < w:
            s = s + pltpu.roll(s, sh, axis=0)
            sh *= 2
        cnt = jnp.minimum(pos + 1, w).astype(F32)
        pooled = s[POOL_HALO:, :] / cnt - a[POOL_HALO:, :]
        mixed = jnp.dot(pooled.astype(BF16), wmix_ref[gi], preferred_element_type=F32)
        parts.append(mixed * pscale_ref[:, lo:hi])
    ubuf[0:POOL_HALO, :] = u[tm - POOL_HALO:, :]
    p_branch = jnp.dot(jnp.concatenate(parts, axis=-1).astype(BF16), wpbr_ref[...],
                       preferred_element_type=F32)

    g_a = _sigmoid(proj(wg_ref, 0, d_model) + bg_ref[:, 0:d_model])
    ga_ref[...] = g_a.astype(BF16)
    g_p = _sigmoid(proj(wg_ref, d_model, 2 * d_model) + bg_ref[:, d_model:2 * d_model])
    gp_ref[...] = (g_p * p_branch).astype(BF16)


def _inproj(x2, wq_t, wk_aug, wv_t, wu, wg, kbias, b_gate, w_mix, pool_scale, w_pool_br, bsz, seq):
    t, d = x2.shape
    tm = ATTN_TILE
    n_t = seq // tm
    row = lambda i: (i, 0)
    full2 = lambda i: (0, 0)
    blk4 = lambda i: (i // n_t, i % n_t, 0, 0)
    kern = functools.partial(_inproj_kernel, tiles_per_seq=n_t, tm=tm, d_model=d)
    ka_w = wk_aug.shape[1]
    return pl.pallas_call(
        kern,
        grid=(t // tm,),
        in_specs=[
            pl.BlockSpec((tm, d), row),
            pl.BlockSpec(wq_t.shape, full2),
            pl.BlockSpec(wk_aug.shape, full2),
            pl.BlockSpec(wv_t.shape, full2),
            pl.BlockSpec(wu.shape, full2),
            pl.BlockSpec(wg.shape, full2),
            pl.BlockSpec((tm, ka_w), lambda i: (i % n_t, 0)),
            pl.BlockSpec((1, 2 * d), full2),
            pl.BlockSpec(w_mix.shape, lambda i: (0, 0, 0)),
            pl.BlockSpec((1, POOL_WIDTH), full2),
            pl.BlockSpec((POOL_WIDTH, d), full2),
        ],
        out_specs=[
            pl.BlockSpec((None, None, ATTN_WIDTH, tm), blk4),
            pl.BlockSpec((None, None, tm, ka_w), blk4),
            pl.BlockSpec((None, None, ATTN_WIDTH, tm), blk4),
            pl.BlockSpec((tm, d), row),
            pl.BlockSpec((tm, d), row),
        ],
        out_shape=[
            jax.ShapeDtypeStruct((bsz, n_t, ATTN_WIDTH, tm), BF16),
            jax.ShapeDtypeStruct((bsz, n_t, tm, ka_w), BF16),
            jax.ShapeDtypeStruct((bsz, n_t, ATTN_WIDTH, tm), BF16),
            jax.ShapeDtypeStruct((t, d), BF16),
            jax.ShapeDtypeStruct((t, d), BF16),
        ],
        scratch_shapes=[pltpu.VMEM((POOL_HALO + tm, POOL_WIDTH), F32)],
        compiler_params=_params(),
        name="inproj",
    )(x2, wq_t, wk_aug, wv_t, wu, wg, kbias, b_gate, w_mix, pool_scale, w_pool_br)


def _attn_kernel(lvec_ref, sg_ref, qt_ref, ka_ref, vt_ref, o_ref, m_scr, acc_scr, *, tq, lam_init):
    qi = pl.program_id(2)
    hd = DIFF_HEAD_DIM
    qt = qt_ref[...]
    sub = lax.broadcasted_iota(I32, (hd, tq), 0)
    ones_rows = jnp.where(sub < N_BIAS_PIECES, 1.0, 0.0).astype(BF16)
    qa = (jnp.concatenate([qt[0:hd], ones_rows], axis=0), jnp.concatenate([qt[hd:2 * hd], ones_rows], axis=0))

    m_scr[...] = jnp.full(m_scr.shape, -jnp.inf, F32)
    acc_scr[...] = jnp.zeros(acc_scr.shape, F32)
    sum_rows = jnp.ones((SUM_ROWS, tq), BF16)

    def step(ki, masked):
        vt = jnp.concatenate([vt_ref[ki], sum_rows], axis=0)
        zs = []
        for m in range(2):
            ka = ka_ref[ki, :, m * HEAD_WIDTH:(m + 1) * HEAD_WIDTH]
            z = jnp.dot(ka, qa[m], preferred_element_type=F32)
            if masked:
                r = lax.broadcasted_iota(I32, z.shape, 0)
                c = lax.broadcasted_iota(I32, z.shape, 1)
                z = jnp.where(r <= c, z, -jnp.inf)
            zs.append(z)
        for m in range(2):
            z = zs[m]
            m_old = m_scr[m]
            m_new = jnp.maximum(m_old, jnp.max(z, axis=0, keepdims=True))
            alpha = jnp.exp2(m_old - m_new)
            p = jnp.exp2(z - m_new).astype(BF16)
            acc_scr[m] = alpha * acc_scr[m] + jnp.dot(vt, p, preferred_element_type=F32)
            m_scr[m] = m_new

    def body(ki, carry):
        step(ki, False)
        return carry

    lax.fori_loop(0, qi, body, 0)
    step(qi, True)

    lv = lvec_ref[...]
    lam = (jnp.exp(jnp.sum(lv[0:1] * lv[1:2], axis=-1, keepdims=True))
           - jnp.exp(jnp.sum(lv[2:3] * lv[3:4], axis=-1, keepdims=True)) + lam_init)
    hw = HEAD_WIDTH
    o = (acc_scr[0, 0:hw] / acc_scr[0, hw:hw + 1]
         - lam * (acc_scr[1, 0:hw] / acc_scr[1, hw:hw + 1]))
    ms = jnp.mean(o * o, axis=0, keepdims=True)
    o = o * lax.rsqrt(ms + LN_EPS) * sg_ref[...] * (1.0 - lam_init)
    o_ref[...] = o.T.astype(BF16)


def _attention(qt, kaug, vt, lvec, subln_g_col, lam_init):
    b, n_t, _, tq = qt.shape
    kern = functools.partial(_attn_kernel, tq=tq, lam_init=lam_init)
    return pl.pallas_call(
        kern,
        grid=(b, N_DIFF_HEADS, n_t),
        in_specs=[
            pl.BlockSpec(lvec.shape, lambda bi, h, i: (0, 0)),
            pl.BlockSpec((HEAD_WIDTH, 1), lambda bi, h, i: (0, 0)),
            pl.BlockSpec((None, None, HEAD_WIDTH, tq), lambda bi, h, i: (bi, i, h, 0)),
            pl.BlockSpec((None, n_t, tq, 2 * HEAD_WIDTH), lambda bi, h, i: (bi, 0, 0, h)),
            pl.BlockSpec((None, n_t, HEAD_WIDTH, tq), lambda bi, h, i: (bi, 0, h, 0)),
        ],
        out_specs=pl.BlockSpec((None, tq, HEAD_WIDTH), lambda bi, h, i: (bi, i, h)),
        out_shape=jax.ShapeDtypeStruct((b, n_t * tq, ATTN_WIDTH), BF16),
        scratch_shapes=[
            pltpu.VMEM((2, 1, tq), F32),
            pltpu.VMEM((2, HEAD_WIDTH + SUM_ROWS, tq), F32),
        ],
        compiler_params=_params(3),
        name="diffattn",
    )(lvec, subln_g_col, qt, kaug, vt)


def _mix_kernel(o_ref, ga_ref, gp_ref, x_ref, wabr_ref, wout_ref, g1_ref, b1_ref, wrh_ref, wrl_ref, br_ref,
                x1_ref, x1t_ref, topi_ref, rank_ref, gate_ref, cnt_ref, carry, *, tm, alpha):
    i = pl.program_id(0)

    @pl.when(i == 0)
    def _():
        carry[...] = jnp.zeros(carry.shape, F32)

    a_branch = jnp.dot(o_ref[...], wabr_ref[...], preferred_element_type=F32)
    mixed = ga_ref[...].astype(F32) * a_branch + gp_ref[...].astype(F32)
    y = jnp.dot(mixed.astype(BF16), wout_ref[...], preferred_element_type=F32)
    x1 = _layer_norm(alpha * x_ref[...] + y, g1_ref[...], b1_ref[...])
    x1_ref[...] = x1
    _store_token_tiles(x1t_ref, x1)

    x_hi = x1.astype(BF16)
    x_lo = (x1 - x_hi.astype(F32)).astype(BF16)
    logits = (jnp.dot(x_hi, wrh_ref[...], preferred_element_type=F32)
              + jnp.dot(x_lo, wrh_ref[...], preferred_element_type=F32)
              + jnp.dot(x_hi, wrl_ref[...], preferred_element_type=F32)) + br_ref[...]
    e_iota = lax.broadcasted_iota(I32, logits.shape, 1).astype(F32)
    vals, idxs = [], []
    rem = logits
    for _ in range(TOP_K):
        m = jnp.max(rem, axis=-1, keepdims=True)
        idx = jnp.min(jnp.where(rem == m, e_iota, float(N_EXPERTS)), axis=-1, keepdims=True)
        vals.append(m)
        idxs.append(idx)
        rem = jnp.where(e_iota == idx, -jnp.inf, rem)
    exps = [jnp.exp(vk - vals[0]) for vk in vals]
    den = exps[0] + exps[1] + exps[2] + exps[3]

    sel = jnp.zeros(logits.shape, F32)
    for idx in idxs:
        sel = sel + jnp.where(e_iota == idx, 1.0, 0.0)
    r = lax.broadcasted_iota(I32, (tm, tm), 0)
    c = lax.broadcasted_iota(I32, (tm, tm), 1)
    tri = jnp.where(c < r, 1.0, 0.0).astype(BF16)
    prefix = jnp.dot(tri, sel.astype(BF16), preferred_element_type=F32) + carry[...]
    carry[...] = carry[...] + jnp.sum(sel, axis=0, keepdims=True)
    cnt_ref[...] = carry[...]

    lane = lax.broadcasted_iota(I32, (tm, LANES), 1)
    topi = jnp.zeros((tm, LANES), F32)
    rank = jnp.zeros((tm, LANES), F32)
    gate = jnp.zeros((tm, LANES), F32)
    for kk in range(TOP_K):
        rk = jnp.sum(jnp.where(e_iota == idxs[kk], prefix, 0.0), axis=-1, keepdims=True)
        topi = jnp.where(lane == kk, idxs[kk], topi)
        rank = jnp.where(lane == kk, rk, rank)
        gate = jnp.where(lane == kk, exps[kk] / den, gate)
    topi_ref[...] = topi.astype(I32)
    rank_ref[...] = rank.astype(I32)
    gate_ref[...] = gate


def _mix(o2, ga, gp, x2, w_attn_br, w_out, ln_g, ln_b, w_router_hi, w_router_lo, b_router, alpha):
    t, d = x2.shape
    tm = TOKEN_TILE
    row = lambda i: (i, 0)
    full2 = lambda i: (0, 0)
    kern = functools.partial(_mix_kernel, tm=tm, alpha=alpha)
    return pl.pallas_call(
        kern,
        grid=(t // tm,),
        in_specs=[
            pl.BlockSpec((tm, ATTN_WIDTH), row),
            pl.BlockSpec((tm, d), row),
            pl.BlockSpec((tm, d), row),
            pl.BlockSpec((tm, d), row),
            pl.BlockSpec((ATTN_WIDTH, d), full2),
            pl.BlockSpec((d, d), full2),
            pl.BlockSpec((1, d), full2),
            pl.BlockSpec((1, d), full2),
            pl.BlockSpec((d, N_EXPERTS), full2),
            pl.BlockSpec((d, N_EXPERTS), full2),
            pl.BlockSpec((1, N_EXPERTS), full2),
        ],
        out_specs=[
            pl.BlockSpec((tm, d), row),
            pl.BlockSpec((tm * ROW_TILE, LANES), row),
            pl.BlockSpec((tm, LANES), row),
            pl.BlockSpec((tm, LANES), row),
            pl.BlockSpec((tm, LANES), row),
            pl.BlockSpec((1, N_EXPERTS), full2),
        ],
        out_shape=[
            jax.ShapeDtypeStruct((t, d), F32),
            jax.ShapeDtypeStruct((t * ROW_TILE, LANES), F32),
            jax.ShapeDtypeStruct((t, LANES), I32),
            jax.ShapeDtypeStruct((t, LANES), I32),
            jax.ShapeDtypeStruct((t, LANES), F32),
            jax.ShapeDtypeStruct((1, N_EXPERTS), F32),
        ],
        scratch_shapes=[pltpu.VMEM((1, N_EXPERTS), F32)],
        compiler_params=_params(),
        name="mix_ln1_router",
    )(o2, ga, gp, x2, w_attn_br, w_out, ln_g, ln_b, w_router_hi, w_router_lo, b_router)


def _store_token_tiles(ref, val):
    rows = val.shape[0]
    for s in range(ROW_TILE):
        ref[pl.ds(s, rows, stride=ROW_TILE), :] = val[:, s * LANES:(s + 1) * LANES]


def _load_token_tiles(ref, rows):
    return jnp.concatenate([ref[pl.ds(s, rows, stride=ROW_TILE), :] for s in range(ROW_TILE)], axis=-1)


def _tile_copy(src_ref, src_row8, dst_ref, dst_row8, sem):
    return pltpu.make_async_copy(src_ref.at[pl.ds(pl.multiple_of(src_row8, ROW_TILE), ROW_TILE), :],
                                 dst_ref.at[pl.ds(pl.multiple_of(dst_row8, ROW_TILE), ROW_TILE), :], sem)


def _dispatch_kernel(dest_ref, x1t_ref, xs_hbm, sem, *, tm):
    def body(t, carry):
        for kk in range(TOP_K):
            _tile_copy(x1t_ref, t * ROW_TILE, xs_hbm, dest_ref[TOP_K * t + kk], sem).start(priority=kk % 2)
        return carry

    lax.fori_loop(0, tm, body, 0, unroll=ISSUE_UNROLL)
    for _ in range(TOP_K):
        pltpu.make_async_copy(x1t_ref, xs_hbm.at[pl.ds(0, tm * ROW_TILE), :], sem).wait()


def _dispatch(dest8, x1t, n_slots):
    tm = TOKEN_TILE
    t = x1t.shape[0] // ROW_TILE
    kern = functools.partial(_dispatch_kernel, tm=tm)
    return pl.pallas_call(
        kern,
        grid=(t // tm,),
        in_specs=[
            pl.BlockSpec((TOP_K * tm,), lambda i: (i,), memory_space=pltpu.SMEM),
            pl.BlockSpec((tm * ROW_TILE, LANES), lambda i: (i, 0)),
        ],
        out_specs=pl.BlockSpec(memory_space=pl.ANY),
        out_shape=jax.ShapeDtypeStruct((n_slots * ROW_TILE, LANES), F32),
        scratch_shapes=[pltpu.SemaphoreType.DMA(())],
        compiler_params=_params(),
        name="dispatch",
    )(dest8, x1t)


def _expert_kernel(be_ref, bsrc_ref, bv_ref, xs_ref, wup_ref, bup_ref, wdn_ref, bdn_ref, ys_ref,
                   wup_bf, wdn_bf, *, bm, d_ff):
    b = pl.program_id(0)
    nv = bv_ref[b]

    @pl.when(jnp.logical_or(b == 0, be_ref[b] != be_ref[jnp.maximum(b - 1, 0)]))
    def _():
        wup_bf[...] = wup_ref[0].astype(BF16)
        wdn_bf[...] = wdn_ref[0].astype(BF16)

    @pl.when(nv > 0)
    def _():
        rows = lax.broadcasted_iota(I32, (bm, 1), 0)
        x = jnp.where(rows < nv, _load_token_tiles(xs_ref, bm), 0.0).astype(BF16)
        hid = jnp.dot(x, wup_bf[...], preferred_element_type=F32) + bup_ref[0]
        glu = jnp.minimum(hid[:, :d_ff], SWIGLU_LIMIT)
        lin = jnp.clip(hid[:, d_ff:], -SWIGLU_LIMIT, SWIGLU_LIMIT)
        act = glu * _sigmoid(SWIGLU_ALPHA * glu) * (lin + 1.0)
        y = jnp.dot(act.astype(BF16), wdn_bf[...], preferred_element_type=F32) + bdn_ref[0]
        _store_token_tiles(ys_ref, y)


def _experts(be, bsrc, bv, xs, w_up, b_up, w_down, b_down):
    bm = EXPERT_BLOCK
    n_slots = xs.shape[0] // ROW_TILE
    d_ff, d = w_down.shape[1:]
    kern = functools.partial(_expert_kernel, bm=bm, d_ff=d_ff)
    blk = lambda b, be_r, bsrc_r, bv_r: (bsrc_r[b], 0)
    wsel = lambda b, be_r, bsrc_r, bv_r: (be_r[b], 0, 0)
    grid_spec = pltpu.PrefetchScalarGridSpec(
        num_scalar_prefetch=3,
        grid=(n_slots // bm,),
        in_specs=[
            pl.BlockSpec((bm * ROW_TILE, LANES), blk),
            pl.BlockSpec((1, d, 2 * d_ff), wsel),
            pl.BlockSpec((1, 1, 2 * d_ff), wsel),
            pl.BlockSpec((1, d_ff, d), wsel),
            pl.BlockSpec((1, 1, d), wsel),
        ],
        out_specs=pl.BlockSpec((bm * ROW_TILE, LANES), blk),
        scratch_shapes=[pltpu.VMEM((d, 2 * d_ff), BF16), pltpu.VMEM((d_ff, d), BF16)],
    )
    return pl.pallas_call(
        kern,
        grid_spec=grid_spec,
        out_shape=jax.ShapeDtypeStruct((n_slots * ROW_TILE, LANES), F32),
        compiler_params=_params(),
        name="experts",
    )(be, bsrc, bv, xs, w_up, b_up, w_down, b_down)


def _final_kernel(dest_ref, dnext_ref, x1_ref, p_ref, gate_ref, wg_ref, wp_ref, g2_ref, b2_ref, ys_hbm,
                  o_ref, ybuf, sem, *, tm, alpha):
    i = pl.program_id(0)
    slot = i % 2

    def gather(ids_ref, buf_slot):
        def body(t, carry):
            for kk in range(TOP_K):
                _tile_copy(ys_hbm, ids_ref[TOP_K * t + kk], ybuf.at[buf_slot, kk], t * ROW_TILE,
                           sem.at[buf_slot]).start(priority=kk % 2)
            return carry

        lax.fori_loop(0, tm, body, 0, unroll=ISSUE_UNROLL)

    @pl.when(i == 0)
    def _():
        gather(dest_ref, 0)

    @pl.when(i + 1 < pl.num_programs(0))
    def _():
        gather(dnext_ref, 1 - slot)

    x1 = x1_ref[...]
    ple = (_sigmoid(jnp.dot(x1.astype(BF16), wg_ref[...], preferred_element_type=F32))
           * jnp.dot(p_ref[...].astype(BF16), wp_ref[...], preferred_element_type=F32))

    for kk in range(TOP_K):
        pltpu.make_async_copy(ys_hbm.at[pl.ds(0, tm * ROW_TILE), :], ybuf.at[slot, kk], sem.at[slot]).wait()
    gate = gate_ref[...]
    ffn = gate[:, 0:1] * _load_token_tiles(ybuf.at[slot, 0], tm)
    for kk in range(1, TOP_K):
        ffn = ffn + gate[:, kk:kk + 1] * _load_token_tiles(ybuf.at[slot, kk], tm)
    o_ref[...] = _layer_norm(alpha * x1 + ffn + ple, g2_ref[...], b2_ref[...])


def _final(dest, x1, p2, gate, w_ple_gate, w_ple_proj, ln_g, ln_b, ys, alpha):
    t, d = x1.shape
    tm = COMBINE_TILE
    n_tiles = t // tm
    ple_dim = p2.shape[1]
    row = lambda i: (i, 0)
    full2 = lambda i: (0, 0)
    kern = functools.partial(_final_kernel, tm=tm, alpha=alpha)
    return pl.pallas_call(
        kern,
        grid=(n_tiles,),
        in_specs=[
            pl.BlockSpec((TOP_K * tm,), lambda i: (i,), memory_space=pltpu.SMEM),
            pl.BlockSpec((TOP_K * tm,), lambda i: (jnp.minimum(i + 1, n_tiles - 1),), memory_space=pltpu.SMEM),
            pl.BlockSpec((tm, d), row),
            pl.BlockSpec((tm, ple_dim), row),
            pl.BlockSpec((tm, LANES), row),
            pl.BlockSpec((d, d), full2),
            pl.BlockSpec((ple_dim, d), full2),
            pl.BlockSpec((1, d), full2),
            pl.BlockSpec((1, d), full2),
            pl.BlockSpec(memory_space=pl.ANY),
        ],
        out_specs=pl.BlockSpec((tm, d), row),
        out_shape=jax.ShapeDtypeStruct((t, d), F32),
        scratch_shapes=[pltpu.VMEM((2, TOP_K, tm * ROW_TILE, LANES), F32), pltpu.SemaphoreType.DMA((2,))],
        compiler_params=_params(),
        name="combine_ple_ln2",
    )(dest, dest, x1, p2, gate, w_ple_gate, w_ple_proj, ln_g, ln_b, ys)


def _augment_k_weight(wk):
    d = wk.shape[0]
    w = wk.reshape(d, N_DIFF_HEADS * 2, DIFF_HEAD_DIM)
    w = jnp.pad(w, ((0, 0), (0, 0), (0, HEAD_WIDTH - DIFF_HEAD_DIM)))
    return w.reshape(d, N_DIFF_HEADS * 2 * HEAD_WIDTH)


def _alibi_key_bias(seq):
    slopes = np.asarray([2.0 ** (-8.0 * (h + 1) / N_DIFF_HEADS) for h in range(N_DIFF_HEADS)], np.float32)
    b = slopes[None, :] * np.arange(seq, dtype=np.float32)[:, None] * np.float32(LOG2E)
    pieces = []
    rem = b
    for _ in range(N_BIAS_PIECES):
        piece = (rem.view(np.uint32) & np.uint32(0xFFFF0000)).view(np.float32)
        pieces.append(piece)
        rem = rem - piece
    pb = jnp.asarray(np.stack(pieces, axis=-1)).astype(BF16)
    pb = jnp.broadcast_to(pb[:, :, None, :], (seq, N_DIFF_HEADS, 2, N_BIAS_PIECES))
    pb = jnp.pad(pb, ((0, 0), (0, 0), (0, 0), (DIFF_HEAD_DIM, HEAD_WIDTH - DIFF_HEAD_DIM - N_BIAS_PIECES)))
    return pb.reshape(seq, N_DIFF_HEADS * 2 * HEAD_WIDTH)


def _split_bf16(w):
    hi = lax.bitcast_convert_type(lax.bitcast_convert_type(w, jnp.uint32) & jnp.uint32(0xFFFF0000), F32)
    return hi.astype(BF16), (w - hi).astype(BF16)


def _routing_tables(counts, topi, rank, n_blocks):
    bm = EXPERT_BLOCK
    n_blk_e = (counts + bm - 1) // bm
    cum_blk = jnp.cumsum(n_blk_e)
    blk_start = cum_blk - n_blk_e
    n_used = cum_blk[-1]
    experts = jnp.arange(N_EXPERTS, dtype=I32)
    slot_start = jnp.sum(jnp.where(topi[..., None] == experts, blk_start * bm, 0), axis=-1)
    dest = (slot_start + rank).reshape(-1).astype(I32)
    bidx = jnp.arange(n_blocks, dtype=I32)
    bsrc = jnp.minimum(bidx, n_used - 1).astype(I32)
    be = jnp.minimum(jnp.sum((cum_blk[None, :] <= bsrc[:, None]).astype(I32), axis=-1), N_EXPERTS - 1)
    valid = jnp.clip(counts[be] - (bsrc - blk_start[be]) * bm, 0, bm)
    bv = jnp.where(bidx < n_used, valid, 0).astype(I32)
    return dest, be, bsrc, bv


def kernel(x, p, w_in, b_gate, lambda_q1, lambda_k1, lambda_q2, lambda_k2, subln_g, w_attn_br,
           w_pool_mix, pool_scale, w_pool_br, w_out, ln1_g, ln1_b, w_router, b_router, w_up, b_up,
           w_down, b_down, w_ple_gate, w_ple_proj, ln2_g, ln2_b):
    bsz, seq, d = x.shape
    depth = w_in.shape[0]
    t = bsz * seq
    assert seq % ATTN_TILE == 0 and seq % TOKEN_TILE == 0 and t % EXPERT_BLOCK == 0
    alpha = (2 * depth) ** 0.25
    n_blocks = (t * TOP_K) // EXPERT_BLOCK + N_EXPERTS
    n_slots = n_blocks * EXPERT_BLOCK

    x2 = x.reshape(t, d)
    for i in range(depth):
        lam_init = 0.8 - 0.6 * math.exp(-0.3 * i)
        wq, wk, wv, wu, wg = jnp.split(w_in[i].astype(BF16), IN_SPLITS, axis=-1)
        qt, kaug, vt, ga, gp = _inproj(x2, wq.T, _augment_k_weight(wk), wv.T, wu, wg, _alibi_key_bias(seq),
                                       b_gate[i][None], w_pool_mix[i].astype(BF16), pool_scale[i][None],
                                       w_pool_br[i].astype(BF16), bsz, seq)
        lvec = jnp.stack([lambda_q1[i], lambda_k1[i], lambda_q2[i], lambda_k2[i]]).astype(F32)
        o = _attention(qt, kaug, vt, lvec, subln_g[i][:, None], lam_init)
        x1, x1t, topi, rank, gate, cnt = _mix(o.reshape(t, ATTN_WIDTH), ga, gp, x2,
                                         w_attn_br[i].astype(BF16), w_out[i].astype(BF16),
                                         ln1_g[i][None], ln1_b[i][None], *_split_bf16(w_router[i]),
                                         b_router[i][None], alpha)
        counts = cnt[0].astype(I32)
        dest, be, bsrc, bv = _routing_tables(counts, topi[:, :TOP_K], rank[:, :TOP_K], n_blocks)
        dest8 = dest * ROW_TILE
        xs = _dispatch(dest8, x1t, n_slots)
        ys = _experts(be, bsrc, bv, xs, w_up[i], b_up[i][:, None, :], w_down[i], b_down[i][:, None, :])
        x2 = _final(dest8, x1, p[i].reshape(t, -1), gate, w_ple_gate[i].astype(BF16),
                    w_ple_proj[i].astype(BF16), ln2_g[i][None], ln2_b[i][None], ys, alpha)
    return x2.reshape(bsz, seq, d)
```

```python
import functools
import math

import jax
import jax.numpy as jnp
import numpy as np
from jax import lax
from jax.experimental import pallas as pl
from jax.experimental.pallas import tpu as pltpu

F32 = jnp.float32
BF16 = jnp.bfloat16
I32 = jnp.int32

N_DIFF_HEADS = 4
DIFF_HEAD_DIM = 64
HEAD_WIDTH = 2 * DIFF_HEAD_DIM
ATTN_WIDTH = N_DIFF_HEADS * HEAD_WIDTH
POOL_WINDOWS = (2, 4, 8, 16)
POOL_GROUP_DIM = 128
POOL_WIDTH = len(POOL_WINDOWS) * POOL_GROUP_DIM
POOL_HALO = 16
N_EXPERTS = 32
TOP_K = 4
SWIGLU_LIMIT = 7.0
SWIGLU_ALPHA = 1.702
LN_EPS = 1e-5
LANES = 128
ROW_TILE = 8
LOG2E = math.log2(math.e)
N_BIAS_PIECES = 3
IN_SPLITS = (ATTN_WIDTH, 2 * ATTN_WIDTH, 3 * ATTN_WIDTH, 3 * ATTN_WIDTH + POOL_WIDTH)

TOKEN_TILE = 512
ATTN_TILE = 512
SUM_ROWS = 16
EXPERT_BLOCK = 512
COMBINE_TILE = 512
ISSUE_UNROLL = 4
VMEM_LIMIT = 56 * 1024 * 1024


def _sigmoid(z):
    return 1.0 / (1.0 + jnp.exp(-z))


def _layer_norm(z, g, b):
    mu = jnp.mean(z, axis=-1, keepdims=True)
    zc = z - mu
    var = jnp.mean(zc * zc, axis=-1, keepdims=True)
    return zc * lax.rsqrt(var + LN_EPS) * g + b


def _params(n_axes=1):
    return pltpu.CompilerParams(dimension_semantics=("arbitrary",) * n_axes,
                                vmem_limit_bytes=VMEM_LIMIT)


def _inproj_kernel(x_ref, wq_ref, wk_ref, wv_ref, wu_ref, wg_ref, kb_ref, bg_ref, wmix_ref, pscale_ref,
                   wpbr_ref, qt_ref, ka_ref, vt_ref, ga_ref, gp_ref, ubuf, *, tiles_per_seq, tm, d_model):
    j = pl.program_id(0) % tiles_per_seq
    xb = x_ref[...].astype(BF16)

    def proj(w_ref, lo=None, hi=None):
        w = w_ref[...] if lo is None else w_ref[:, lo:hi]
        return jnp.dot(xb, w, preferred_element_type=F32)

    def proj_t(wt_ref):
        return lax.dot_general(wt_ref[...], xb, (((1,), (1,)), ((), ())), preferred_element_type=F32)

    qt_ref[...] = (proj_t(wq_ref) * (DIFF_HEAD_DIM ** -0.5 * LOG2E)).astype(BF16)
    ka_ref[...] = (proj(wk_ref) + kb_ref[...].astype(F32)).astype(BF16)
    vt_ref[...] = proj_t(wv_ref).astype(BF16)
    u = proj(wu_ref)

    @pl.when(j == 0)
    def _():
        ubuf[0:POOL_HALO, :] = jnp.zeros((POOL_HALO, POOL_WIDTH), F32)

    ubuf[POOL_HALO:, :] = u
    pos = j * tm + lax.broadcasted_iota(I32, (tm, 1), 0)
    parts = []
    for gi, w in enumerate(POOL_WINDOWS):
        lo, hi = gi * POOL_GROUP_DIM, (gi + 1) * POOL_GROUP_DIM
        a = ubuf[:, lo:hi]
        s = a
        sh = 1
        while sh < w:
            s = s + pltpu.roll(s, sh, axis=0)
            sh *= 2
        cnt = jnp.minimum(pos + 1, w).astype(F32)
        pooled = s[POOL_HALO:, :] / cnt - a[POOL_HALO:, :]
        mixed = jnp.dot(pooled.astype(BF16), wmix_ref[gi], preferred_element_type=F32)
        parts.append(mixed * pscale_ref[:, lo:hi])
    ubuf[0:POOL_HALO, :] = u[tm - POOL_HALO:, :]
    p_branch = jnp.dot(jnp.concatenate(parts, axis=-1).astype(BF16), wpbr_ref[...],
                       preferred_element_type=F32)

    g_a = _sigmoid(proj(wg_ref, 0, d_model) + bg_ref[:, 0:d_model])
    ga_ref[...] = g_a.astype(BF16)
    g_p = _sigmoid(proj(wg_ref, d_model, 2 * d_model) + bg_ref[:, d_model:2 * d_model])
    gp_ref[...] = (g_p * p_branch).astype(BF16)


def _inproj(x2, wq_t, wk_aug, wv_t, wu, wg, kbias, b_gate, w_mix, pool_scale, w_pool_br, bsz, seq):
    t, d = x2.shape
    tm = ATTN_TILE
    n_t = seq // tm
    row = lambda i: (i, 0)
    full2 = lambda i: (0, 0)
    blk4 = lambda i: (i // n_t, i % n_t, 0, 0)
    kern = functools.partial(_inproj_kernel, tiles_per_seq=n_t, tm=tm, d_model=d)
    ka_w = wk_aug.shape[1]
    return pl.pallas_call(
        kern,
        grid=(t // tm,),
        in_specs=[
            pl.BlockSpec((tm, d), row),
            pl.BlockSpec(wq_t.shape, full2),
            pl.BlockSpec(wk_aug.shape, full2),
            pl.BlockSpec(wv_t.shape, full2),
            pl.BlockSpec(wu.shape, full2),
            pl.BlockSpec(wg.shape, full2),
            pl.BlockSpec((tm, ka_w), lambda i: (i % n_t, 0)),
            pl.BlockSpec((1, 2 * d), full2),
            pl.BlockSpec(w_mix.shape, lambda i: (0, 0, 0)),
            pl.BlockSpec((1, POOL_WIDTH), full2),
            pl.BlockSpec((POOL_WIDTH, d), full2),
        ],
        out_specs=[
            pl.BlockSpec((None, None, ATTN_WIDTH, tm), blk4),
            pl.BlockSpec((None, None, tm, ka_w), blk4),
            pl.BlockSpec((None, None, ATTN_WIDTH, tm), blk4),
            pl.BlockSpec((tm, d), row),
            pl.BlockSpec((tm, d), row),
        ],
        out_shape=[
            jax.ShapeDtypeStruct((bsz, n_t, ATTN_WIDTH, tm), BF16),
            jax.ShapeDtypeStruct((bsz, n_t, tm, ka_w), BF16),
            jax.ShapeDtypeStruct((bsz, n_t, ATTN_WIDTH, tm), BF16),
            jax.ShapeDtypeStruct((t, d), BF16),
            jax.ShapeDtypeStruct((t, d), BF16),
        ],
        scratch_shapes=[pltpu.VMEM((POOL_HALO + tm, POOL_WIDTH), F32)],
        compiler_params=_params(),
        name="inproj",
    )(x2, wq_t, wk_aug, wv_t, wu, wg, kbias, b_gate, w_mix, pool_scale, w_pool_br)


def _attn_kernel(lvec_ref, sg_ref, qt_ref, ka_ref, vt_ref, o_ref, m_scr, acc_scr, za_scr, zb_scr,
                 *, tq, lam_init):
    qi = pl.program_id(2)
    hd = DIFF_HEAD_DIM
    qt = qt_ref[...]
    sub = lax.broadcasted_iota(I32, (hd, tq), 0)
    ones_rows = jnp.where(sub < N_BIAS_PIECES, 1.0, 0.0).astype(BF16)
    qa = (jnp.concatenate([qt[0:hd], ones_rows], axis=0), jnp.concatenate([qt[hd:2 * hd], ones_rows], axis=0))

    m_scr[...] = jnp.full(m_scr.shape, -jnp.inf, F32)
    acc_scr[...] = jnp.zeros(acc_scr.shape, F32)
    sum_rows = jnp.ones((SUM_ROWS, tq), BF16)

    def scores(ki, z_ref):
        for m in range(2):
            ka = ka_ref[ki, :, m * HEAD_WIDTH:(m + 1) * HEAD_WIDTH]
            z_ref[m] = jnp.dot(ka, qa[m], preferred_element_type=F32)

    def consume(ki, z_ref, masked):
        vt = jnp.concatenate([vt_ref[ki], sum_rows], axis=0)
        for m in range(2):
            z = z_ref[m]
            if masked:
                r = lax.broadcasted_iota(I32, z.shape, 0)
                c = lax.broadcasted_iota(I32, z.shape, 1)
                z = jnp.where(r <= c, z, -jnp.inf)
            m_old = m_scr[m]
            m_new = jnp.maximum(m_old, jnp.max(z, axis=0, keepdims=True))
            alpha = jnp.exp2(m_old - m_new)
            p = jnp.exp2(z - m_new).astype(BF16)
            acc_scr[m] = alpha * acc_scr[m] + jnp.dot(vt, p, preferred_element_type=F32)
            m_scr[m] = m_new

    scores(0, za_scr)

    def pair(j, carry):
        ki = 2 * j
        scores(ki + 1, zb_scr)
        consume(ki, za_scr, False)
        scores(ki + 2, za_scr)
        consume(ki + 1, zb_scr, False)
        return carry

    lax.fori_loop(0, qi // 2, pair, 0)

    @pl.when(qi % 2 == 0)
    def _():
        consume(qi, za_scr, True)

    @pl.when(qi % 2 == 1)
    def _():
        scores(qi, zb_scr)
        consume(qi - 1, za_scr, False)
        consume(qi, zb_scr, True)

    lv = lvec_ref[...]
    lam = (jnp.exp(jnp.sum(lv[0:1] * lv[1:2], axis=-1, keepdims=True))
           - jnp.exp(jnp.sum(lv[2:3] * lv[3:4], axis=-1, keepdims=True)) + lam_init)
    hw = HEAD_WIDTH
    o = (acc_scr[0, 0:hw] / acc_scr[0, hw:hw + 1]
         - lam * (acc_scr[1, 0:hw] / acc_scr[1, hw:hw + 1]))
    ms = jnp.mean(o * o, axis=0, keepdims=True)
    o = o * lax.rsqrt(ms + LN_EPS) * sg_ref[...] * (1.0 - lam_init)
    o_ref[...] = o.T.astype(BF16)


def _attention(qt, kaug, vt, lvec, subln_g_col, lam_init):
    b, n_t, _, tq = qt.shape
    kern = functools.partial(_attn_kernel, tq=tq, lam_init=lam_init)
    return pl.pallas_call(
        kern,
        grid=(b, N_DIFF_HEADS, n_t),
        in_specs=[
            pl.BlockSpec(lvec.shape, lambda bi, h, i: (0, 0)),
            pl.BlockSpec((HEAD_WIDTH, 1), lambda bi, h, i: (0, 0)),
            pl.BlockSpec((None, None, HEAD_WIDTH, tq), lambda bi, h, i: (bi, i, h, 0)),
            pl.BlockSpec((None, n_t, tq, 2 * HEAD_WIDTH), lambda bi, h, i: (bi, 0, 0, h)),
            pl.BlockSpec((None, n_t, HEAD_WIDTH, tq), lambda bi, h, i: (bi, 0, h, 0)),
        ],
        out_specs=pl.BlockSpec((None, tq, HEAD_WIDTH), lambda bi, h, i: (bi, i, h)),
        out_shape=jax.ShapeDtypeStruct((b, n_t * tq, ATTN_WIDTH), BF16),
        scratch_shapes=[
            pltpu.VMEM((2, 1, tq), F32),
            pltpu.VMEM((2, HEAD_WIDTH + SUM_ROWS, tq), F32),
            pltpu.VMEM((2, tq, tq), F32),
            pltpu.VMEM((2, tq, tq), F32),
        ],
        compiler_params=_params(3),
        name="diffattn",
    )(lvec, subln_g_col, qt, kaug, vt)


def _mix_kernel(o_ref, ga_ref, gp_ref, x_ref, wabr_ref, wout_ref, g1_ref, b1_ref, wrh_ref, wrl_ref, br_ref,
                x1_ref, x1t_ref, topi_ref, rank_ref, gate_ref, cnt_ref, carry, *, tm, alpha):
    i = pl.program_id(0)

    @pl.when(i == 0)
    def _():
        carry[...] = jnp.zeros(carry.shape, F32)

    a_branch = jnp.dot(o_ref[...], wabr_ref[...], preferred_element_type=F32)
    mixed = ga_ref[...].astype(F32) * a_branch + gp_ref[...].astype(F32)
    y = jnp.dot(mixed.astype(BF16), wout_ref[...], preferred_element_type=F32)
    x1 = _layer_norm(alpha * x_ref[...] + y, g1_ref[...], b1_ref[...])
    x1_ref[...] = x1
    _store_token_tiles(x1t_ref, x1)

    x_hi = x1.astype(BF16)
    x_lo = (x1 - x_hi.astype(F32)).astype(BF16)
    logits = (jnp.dot(x_hi, wrh_ref[...], preferred_element_type=F32)
              + jnp.dot(x_lo, wrh_ref[...], preferred_element_type=F32)
              + jnp.dot(x_hi, wrl_ref[...], preferred_element_type=F32)) + br_ref[...]
    e_iota = lax.broadcasted_iota(I32, logits.shape, 1).astype(F32)
    vals, idxs = [], []
    rem = logits
    for _ in range(TOP_K):
        m = jnp.max(rem, axis=-1, keepdims=True)
        idx = jnp.min(jnp.where(rem == m, e_iota, float(N_EXPERTS)), axis=-1, keepdims=True)
        vals.append(m)
        idxs.append(idx)
        rem = jnp.where(e_iota == idx, -jnp.inf, rem)
    exps = [jnp.exp(vk - vals[0]) for vk in vals]
    den = exps[0] + exps[1] + exps[2] + exps[3]

    sel = jnp.zeros(logits.shape, F32)
    for idx in idxs:
        sel = sel + jnp.where(e_iota == idx, 1.0, 0.0)
    r = lax.broadcasted_iota(I32, (tm, tm), 0)
    c = lax.broadcasted_iota(I32, (tm, tm), 1)
    tri = jnp.where(c < r, 1.0, 0.0).astype(BF16)
    prefix = jnp.dot(tri, sel.astype(BF16), preferred_element_type=F32) + carry[...]
    carry[...] = carry[...] + jnp.sum(sel, axis=0, keepdims=True)
    cnt_ref[...] = carry[...]

    lane = lax.broadcasted_iota(I32, (tm, LANES), 1)
    topi = jnp.zeros((tm, LANES), F32)
    rank = jnp.zeros((tm, LANES), F32)
    gate = jnp.zeros((tm, LANES), F32)
    for kk in range(TOP_K):
        rk = jnp.sum(jnp.where(e_iota == idxs[kk], prefix, 0.0), axis=-1, keepdims=True)
        topi = jnp.where(lane == kk, idxs[kk], topi)
        rank = jnp.where(lane == kk, rk, rank)
        gate = jnp.where(lane == kk, exps[kk] / den, gate)
    topi_ref[...] = topi.astype(I32)
    rank_ref[...] = rank.astype(I32)
    gate_ref[...] = gate


def _mix(o2, ga, gp, x2, w_attn_br, w_out, ln_g, ln_b, w_router_hi, w_router_lo, b_router, alpha):
    t, d = x2.shape
    tm = TOKEN_TILE
    row = lambda i: (i, 0)
    full2 = lambda i: (0, 0)
    kern = functools.partial(_mix_kernel, tm=tm, alpha=alpha)
    return pl.pallas_call(
        kern,
        grid=(t // tm,),
        in_specs=[
            pl.BlockSpec((tm, ATTN_WIDTH), row),
            pl.BlockSpec((tm, d), row),
            pl.BlockSpec((tm, d), row),
            pl.BlockSpec((tm, d), row),
            pl.BlockSpec((ATTN_WIDTH, d), full2),
            pl.BlockSpec((d, d), full2),
            pl.BlockSpec((1, d), full2),
            pl.BlockSpec((1, d), full2),
            pl.BlockSpec((d, N_EXPERTS), full2),
            pl.BlockSpec((d, N_EXPERTS), full2),
            pl.BlockSpec((1, N_EXPERTS), full2),
        ],
        out_specs=[
            pl.BlockSpec((tm, d), row),
            pl.BlockSpec((tm * ROW_TILE, LANES), row),
            pl.BlockSpec((tm, LANES), row),
            pl.BlockSpec((tm, LANES), row),
            pl.BlockSpec((tm, LANES), row),
            pl.BlockSpec((1, N_EXPERTS), full2),
        ],
        out_shape=[
            jax.ShapeDtypeStruct((t, d), F32),
            jax.ShapeDtypeStruct((t * ROW_TILE, LANES), F32),
            jax.ShapeDtypeStruct((t, LANES), I32),
            jax.ShapeDtypeStruct((t, LANES), I32),
            jax.ShapeDtypeStruct((t, LANES), F32),
            jax.ShapeDtypeStruct((1, N_EXPERTS), F32),
        ],
        scratch_shapes=[pltpu.VMEM((1, N_EXPERTS), F32)],
        compiler_params=_params(),
        name="mix_ln1_router",
    )(o2, ga, gp, x2, w_attn_br, w_out, ln_g, ln_b, w_router_hi, w_router_lo, b_router)


def _store_token_tiles(ref, val):
    rows = val.shape[0]
    for s in range(ROW_TILE):
        ref[pl.ds(s, rows, stride=ROW_TILE), :] = val[:, s * LANES:(s + 1) * LANES]


def _load_token_tiles(ref, rows):
    return jnp.concatenate([ref[pl.ds(s, rows, stride=ROW_TILE), :] for s in range(ROW_TILE)], axis=-1)


def _tile_copy(src_ref, src_row8, dst_ref, dst_row8, sem):
    return pltpu.make_async_copy(src_ref.at[pl.ds(pl.multiple_of(src_row8, ROW_TILE), ROW_TILE), :],
                                 dst_ref.at[pl.ds(pl.multiple_of(dst_row8, ROW_TILE), ROW_TILE), :], sem)


def _dispatch_kernel(dest_ref, x1t_ref, xs_hbm, sem, *, tm):
    def body(t, carry):
        for kk in range(TOP_K):
            _tile_copy(x1t_ref, t * ROW_TILE, xs_hbm, dest_ref[TOP_K * t + kk], sem).start(priority=kk % 2)
        return carry

    lax.fori_loop(0, tm, body, 0, unroll=ISSUE_UNROLL)
    for _ in range(TOP_K):
        pltpu.make_async_copy(x1t_ref, xs_hbm.at[pl.ds(0, tm * ROW_TILE), :], sem).wait()


def _dispatch(dest8, x1t, n_slots):
    tm = TOKEN_TILE
    t = x1t.shape[0] // ROW_TILE
    kern = functools.partial(_dispatch_kernel, tm=tm)
    return pl.pallas_call(
        kern,
        grid=(t // tm,),
        in_specs=[
            pl.BlockSpec((TOP_K * tm,), lambda i: (i,), memory_space=pltpu.SMEM),
            pl.BlockSpec((tm * ROW_TILE, LANES), lambda i: (i, 0)),
        ],
        out_specs=pl.BlockSpec(memory_space=pl.ANY),
        out_shape=jax.ShapeDtypeStruct((n_slots * ROW_TILE, LANES), F32),
        scratch_shapes=[pltpu.SemaphoreType.DMA(())],
        compiler_params=_params(),
        name="dispatch",
    )(dest8, x1t)


def _expert_kernel(be_ref, bsrc_ref, bv_ref, xs_ref, wup_ref, bup_ref, wdn_ref, bdn_ref, ys_ref,
                   wup_bf, wdn_bf, *, bm, d_ff):
    b = pl.program_id(0)
    nv = bv_ref[b]

    @pl.when(jnp.logical_or(b == 0, be_ref[b] != be_ref[jnp.maximum(b - 1, 0)]))
    def _():
        wup_bf[...] = wup_ref[0].astype(BF16)
        wdn_bf[...] = wdn_ref[0].astype(BF16)

    @pl.when(nv > 0)
    def _():
        rows = lax.broadcasted_iota(I32, (bm, 1), 0)
        x = jnp.where(rows < nv, _load_token_tiles(xs_ref, bm), 0.0).astype(BF16)
        hid = jnp.dot(x, wup_bf[...], preferred_element_type=F32) + bup_ref[0]
        glu = jnp.minimum(hid[:, :d_ff], SWIGLU_LIMIT)
        lin = jnp.clip(hid[:, d_ff:], -SWIGLU_LIMIT, SWIGLU_LIMIT)
        act = glu * _sigmoid(SWIGLU_ALPHA * glu) * (lin + 1.0)
        y = jnp.dot(act.astype(BF16), wdn_bf[...], preferred_element_type=F32) + bdn_ref[0]
        _store_token_tiles(ys_ref, y)


def _experts(be, bsrc, bv, xs, w_up, b_up, w_down, b_down):
    bm = EXPERT_BLOCK
    n_slots = xs.shape[0] // ROW_TILE
    d_ff, d = w_down.shape[1:]
    kern = functools.partial(_expert_kernel, bm=bm, d_ff=d_ff)
    blk = lambda b, be_r, bsrc_r, bv_r: (bsrc_r[b], 0)
    wsel = lambda b, be_r, bsrc_r, bv_r: (be_r[b], 0, 0)
    grid_spec = pltpu.PrefetchScalarGridSpec(
        num_scalar_prefetch=3,
        grid=(n_slots // bm,),
        in_specs=[
            pl.BlockSpec((bm * ROW_TILE, LANES), blk),
            pl.BlockSpec((1, d, 2 * d_ff), wsel),
            pl.BlockSpec((1, 1, 2 * d_ff), wsel),
            pl.BlockSpec((1, d_ff, d), wsel),
            pl.BlockSpec((1, 1, d), wsel),
        ],
        out_specs=pl.BlockSpec((bm * ROW_TILE, LANES), blk),
        scratch_shapes=[pltpu.VMEM((d, 2 * d_ff), BF16), pltpu.VMEM((d_ff, d), BF16)],
    )
    return pl.pallas_call(
        kern,
        grid_spec=grid_spec,
        out_shape=jax.ShapeDtypeStruct((n_slots * ROW_TILE, LANES), F32),
        compiler_params=_params(),
        name="experts",
    )(be, bsrc, bv, xs, w_up, b_up, w_down, b_down)


def _final_kernel(dest_ref, dnext_ref, x1_ref, p_ref, gate_ref, wg_ref, wp_ref, g2_ref, b2_ref, ys_hbm,
                  o_ref, ybuf, sem, *, tm, alpha):
    i = pl.program_id(0)
    slot = i % 2

    def gather(ids_ref, buf_slot):
        def body(t, carry):
            for kk in range(TOP_K):
                _tile_copy(ys_hbm, ids_ref[TOP_K * t + kk], ybuf.at[buf_slot, kk], t * ROW_TILE,
                           sem.at[buf_slot]).start(priority=kk % 2)
            return carry

        lax.fori_loop(0, tm, body, 0, unroll=ISSUE_UNROLL)

    @pl.when(i == 0)
    def _():
        gather(dest_ref, 0)

    @pl.when(i + 1 < pl.num_programs(0))
    def _():
        gather(dnext_ref, 1 - slot)

    x1 = x1_ref[...]
    ple = (_sigmoid(jnp.dot(x1.astype(BF16), wg_ref[...], preferred_element_type=F32))
           * jnp.dot(p_ref[...].astype(BF16), wp_ref[...], preferred_element_type=F32))

    for kk in range(TOP_K):
        pltpu.make_async_copy(ys_hbm.at[pl.ds(0, tm * ROW_TILE), :], ybuf.at[slot, kk], sem.at[slot]).wait()
    gate = gate_ref[...]
    ffn = gate[:, 0:1] * _load_token_tiles(ybuf.at[slot, 0], tm)
    for kk in range(1, TOP_K):
        ffn = ffn + gate[:, kk:kk + 1] * _load_token_tiles(ybuf.at[slot, kk], tm)
    o_ref[...] = _layer_norm(alpha * x1 + ffn + ple, g2_ref[...], b2_ref[...])


def _final(dest, x1, p2, gate, w_ple_gate, w_ple_proj, ln_g, ln_b, ys, alpha):
    t, d = x1.shape
    tm = COMBINE_TILE
    n_tiles = t // tm
    ple_dim = p2.shape[1]
    row = lambda i: (i, 0)
    full2 = lambda i: (0, 0)
    kern = functools.partial(_final_kernel, tm=tm, alpha=alpha)
    return pl.pallas_call(
        kern,
        grid=(n_tiles,),
        in_specs=[
            pl.BlockSpec((TOP_K * tm,), lambda i: (i,), memory_space=pltpu.SMEM),
            pl.BlockSpec((TOP_K * tm,), lambda i: (jnp.minimum(i + 1, n_tiles - 1),), memory_space=pltpu.SMEM),
            pl.BlockSpec((tm, d), row),
            pl.BlockSpec((tm, ple_dim), row),
            pl.BlockSpec((tm, LANES), row),
            pl.BlockSpec((d, d), full2),
            pl.BlockSpec((ple_dim, d), full2),
            pl.BlockSpec((1, d), full2),
            pl.BlockSpec((1, d), full2),
            pl.BlockSpec(memory_space=pl.ANY),
        ],
        out_specs=pl.BlockSpec((tm, d), row),
        out_shape=jax.ShapeDtypeStruct((t, d), F32),
        scratch_shapes=[pltpu.VMEM((2, TOP_K, tm * ROW_TILE, LANES), F32), pltpu.SemaphoreType.DMA((2,))],
        compiler_params=_params(),
        name="combine_ple_ln2",
    )(dest, dest, x1, p2, gate, w_ple_gate, w_ple_proj, ln_g, ln_b, ys)


def _augment_k_weight(wk):
    d = wk.shape[0]
    w = wk.reshape(d, N_DIFF_HEADS * 2, DIFF_HEAD_DIM)
    w = jnp.pad(w, ((0, 0), (0, 0), (0, HEAD_WIDTH - DIFF_HEAD_DIM)))
    return w.reshape(d, N_DIFF_HEADS * 2 * HEAD_WIDTH)


def _alibi_key_bias(seq):
    slopes = np.asarray([2.0 ** (-8.0 * (h + 1) / N_DIFF_HEADS) for h in range(N_DIFF_HEADS)], np.float32)
    b = slopes[None, :] * np.arange(seq, dtype=np.float32)[:, None] * np.float32(LOG2E)
    pieces = []
    rem = b
    for _ in range(N_BIAS_PIECES):
        piece = (rem.view(np.uint32) & np.uint32(0xFFFF0000)).view(np.float32)
        pieces.append(piece)
        rem = rem - piece
    pb = jnp.asarray(np.stack(pieces, axis=-1)).astype(BF16)
    pb = jnp.broadcast_to(pb[:, :, None, :], (seq, N_DIFF_HEADS, 2, N_BIAS_PIECES))
    pb = jnp.pad(pb, ((0, 0), (0, 0), (0, 0), (DIFF_HEAD_DIM, HEAD_WIDTH - DIFF_HEAD_DIM - N_BIAS_PIECES)))
    return pb.reshape(seq, N_DIFF_HEADS * 2 * HEAD_WIDTH)


def _split_bf16(w):
    hi = lax.bitcast_convert_type(lax.bitcast_convert_type(w, jnp.uint32) & jnp.uint32(0xFFFF0000), F32)
    return hi.astype(BF16), (w - hi).astype(BF16)


def _routing_tables(counts, topi, rank, n_blocks):
    bm = EXPERT_BLOCK
    n_blk_e = (counts + bm - 1) // bm
    cum_blk = jnp.cumsum(n_blk_e)
    blk_start = cum_blk - n_blk_e
    n_used = cum_blk[-1]
    experts = jnp.arange(N_EXPERTS, dtype=I32)
    slot_start = jnp.sum(jnp.where(topi[..., None] == experts, blk_start * bm, 0), axis=-1)
    dest = (slot_start + rank).reshape(-1).astype(I32)
    bidx = jnp.arange(n_blocks, dtype=I32)
    bsrc = jnp.minimum(bidx, n_used - 1).astype(I32)
    be = jnp.minimum(jnp.sum((cum_blk[None, :] <= bsrc[:, None]).astype(I32), axis=-1), N_EXPERTS - 1)
    valid = jnp.clip(counts[be] - (bsrc - blk_start[be]) * bm, 0, bm)
    bv = jnp.where(bidx < n_used, valid, 0).astype(I32)
    return dest, be, bsrc, bv


def kernel(x, p, w_in, b_gate, lambda_q1, lambda_k1, lambda_q2, lambda_k2, subln_g, w_attn_br,
           w_pool_mix, pool_scale, w_pool_br, w_out, ln1_g, ln1_b, w_router, b_router, w_up, b_up,
           w_down, b_down, w_ple_gate, w_ple_proj, ln2_g, ln2_b):
    bsz, seq, d = x.shape
    depth = w_in.shape[0]
    t = bsz * seq
    assert seq % ATTN_TILE == 0 and seq % TOKEN_TILE == 0 and t % EXPERT_BLOCK == 0
    alpha = (2 * depth) ** 0.25
    n_blocks = (t * TOP_K) // EXPERT_BLOCK + N_EXPERTS
    n_slots = n_blocks * EXPERT_BLOCK

    x2 = x.reshape(t, d)
    for i in range(depth):
        lam_init = 0.8 - 0.6 * math.exp(-0.3 * i)
        wq, wk, wv, wu, wg = jnp.split(w_in[i].astype(BF16), IN_SPLITS, axis=-1)
        qt, kaug, vt, ga, gp = _inproj(x2, wq.T, _augment_k_weight(wk), wv.T, wu, wg, _alibi_key_bias(seq),
                                       b_gate[i][None], w_pool_mix[i].astype(BF16), pool_scale[i][None],
                                       w_pool_br[i].astype(BF16), bsz, seq)
        lvec = jnp.stack([lambda_q1[i], lambda_k1[i], lambda_q2[i], lambda_k2[i]]).astype(F32)
        o = _attention(qt, kaug, vt, lvec, subln_g[i][:, None], lam_init)
        x1, x1t, topi, rank, gate, cnt = _mix(o.reshape(t, ATTN_WIDTH), ga, gp, x2,
                                         w_attn_br[i].astype(BF16), w_out[i].astype(BF16),
                                         ln1_g[i][None], ln1_b[i][None], *_split_bf16(w_router[i]),
                                         b_router[i][None], alpha)
        counts = cnt[0].astype(I32)
        dest, be, bsrc, bv = _routing_tables(counts, topi[:, :TOP_K], rank[:, :TOP_K], n_blocks)
        dest8 = dest * ROW_TILE
        xs = _dispatch(dest8, x1t, n_slots)
        ys = _experts(be, bsrc, bv, xs, w_up[i], b_up[i][:, None, :], w_down[i], b_down[i][:, None, :])
        x2 = _final(dest8, x1, p[i].reshape(t, -1), gate, w_ple_gate[i].astype(BF16),
                    w_ple_proj[i].astype(BF16), ln2_g[i][None], ln2_b[i][None], ys, alpha)
    return x2.reshape(bsz, seq, d)
```

```python
import functools
import math

import jax
import jax.numpy as jnp
import numpy as np
from jax import lax
from jax.experimental import pallas as pl
from jax.experimental.pallas import tpu as pltpu

F32 = jnp.float32
BF16 = jnp.bfloat16
I32 = jnp.int32

N_DIFF_HEADS = 4
DIFF_HEAD_DIM = 64
HEAD_WIDTH = 2 * DIFF_HEAD_DIM
ATTN_WIDTH = N_DIFF_HEADS * HEAD_WIDTH
POOL_WINDOWS = (2, 4, 8, 16)
POOL_GROUP_DIM = 128
POOL_WIDTH = len(POOL_WINDOWS) * POOL_GROUP_DIM
POOL_HALO = 16
N_EXPERTS = 32
TOP_K = 4
SWIGLU_LIMIT = 7.0
SWIGLU_ALPHA = 1.702
LN_EPS = 1e-5
LANES = 128
ROW_TILE = 8
LOG2E = math.log2(math.e)
N_BIAS_PIECES = 3
IN_SPLITS = (ATTN_WIDTH, 2 * ATTN_WIDTH, 3 * ATTN_WIDTH, 3 * ATTN_WIDTH + POOL_WIDTH)

TOKEN_TILE = 512
ATTN_TILE = 512
SUM_ROWS = 16
EXPERT_BLOCK = 512
COMBINE_TILE = 512
ISSUE_UNROLL = 4
VMEM_LIMIT = 56 * 1024 * 1024


def _sigmoid(z):
    return 1.0 / (1.0 + jnp.exp(-z))


def _layer_norm(z, g, b):
    mu = jnp.mean(z, axis=-1, keepdims=True)
    zc = z - mu
    var = jnp.mean(zc * zc, axis=-1, keepdims=True)
    return zc * lax.rsqrt(var + LN_EPS) * g + b


def _params(n_axes=1):
    return pltpu.CompilerParams(dimension_semantics=("arbitrary",) * n_axes,
                                vmem_limit_bytes=VMEM_LIMIT)


def _inproj_kernel(x_ref, wq_ref, wk_ref, wv_ref, wu_ref, wg_ref, kb_ref, bg_ref, wmix_ref, pscale_ref,
                   wpbr_ref, qt_ref, ka_ref, vt_ref, ga_ref, gp_ref, ubuf, *, tiles_per_seq, tm, d_model):
    j = pl.program_id(0) % tiles_per_seq
    xb = x_ref[...].astype(BF16)

    def proj(w_ref, lo=None, hi=None):
        w = w_ref[...] if lo is None else w_ref[:, lo:hi]
        return jnp.dot(xb, w, preferred_element_type=F32)

    def proj_t(wt_ref):
        return lax.dot_general(wt_ref[...], xb, (((1,), (1,)), ((), ())), preferred_element_type=F32)

    qt_ref[...] = (proj_t(wq_ref) * (DIFF_HEAD_DIM ** -0.5 * LOG2E)).astype(BF16)
    ka_ref[...] = (proj(wk_ref) + kb_ref[...].astype(F32)).astype(BF16)
    vt_ref[...] = proj_t(wv_ref).astype(BF16)
    u = proj(wu_ref)

    @pl.when(j == 0)
    def _():
        ubuf[0:POOL_HALO, :] = jnp.zeros((POOL_HALO, POOL_WIDTH), F32)

    ubuf[POOL_HALO:, :] = u
    pos = j * tm + lax.broadcasted_iota(I32, (tm, 1), 0)
    parts = []
    for gi, w in enumerate(POOL_WINDOWS):
        lo, hi = gi * POOL_GROUP_DIM, (gi + 1) * POOL_GROUP_DIM
        a = ubuf[:, lo:hi]
        s = a
        sh = 1
        while sh < w:
            s = s + pltpu.roll(s, sh, axis=0)
            sh *= 2
        cnt = jnp.minimum(pos + 1, w).astype(F32)
        pooled = s[POOL_HALO:, :] / cnt - a[POOL_HALO:, :]
        mixed = jnp.dot(pooled.astype(BF16), wmix_ref[gi], preferred_element_type=F32)
        parts.append(mixed * pscale_ref[:, lo:hi])
    ubuf[0:POOL_HALO, :] = u[tm - POOL_HALO:, :]
    p_branch = jnp.dot(jnp.concatenate(parts, axis=-1).astype(BF16), wpbr_ref[...],
                       preferred_element_type=F32)

    g_a = _sigmoid(proj(wg_ref, 0, d_model) + bg_ref[:, 0:d_model])
    ga_ref[...] = g_a.astype(BF16)
    g_p = _sigmoid(proj(wg_ref, d_model, 2 * d_model) + bg_ref[:, d_model:2 * d_model])
    gp_ref[...] = (g_p * p_branch).astype(BF16)


def _inproj(x2, wq_t, wk_aug, wv_t, wu, wg, kbias, b_gate, w_mix, pool_scale, w_pool_br, bsz, seq):
    t, d = x2.shape
    tm = ATTN_TILE
    n_t = seq // tm
    row = lambda i: (i, 0)
    full2 = lambda i: (0, 0)
    blk4 = lambda i: (i // n_t, i % n_t, 0, 0)
    kern = functools.partial(_inproj_kernel, tiles_per_seq=n_t, tm=tm, d_model=d)
    ka_w = wk_aug.shape[1]
    return pl.pallas_call(
        kern,
        grid=(t // tm,),
        in_specs=[
            pl.BlockSpec((tm, d), row),
            pl.BlockSpec(wq_t.shape, full2),
            pl.BlockSpec(wk_aug.shape, full2),
            pl.BlockSpec(wv_t.shape, full2),
            pl.BlockSpec(wu.shape, full2),
            pl.BlockSpec(wg.shape, full2),
            pl.BlockSpec((tm, ka_w), lambda i: (i % n_t, 0)),
            pl.BlockSpec((1, 2 * d), full2),
            pl.BlockSpec(w_mix.shape, lambda i: (0, 0, 0)),
            pl.BlockSpec((1, POOL_WIDTH), full2),
            pl.BlockSpec((POOL_WIDTH, d), full2),
        ],
        out_specs=[
            pl.BlockSpec((None, None, ATTN_WIDTH, tm), blk4),
            pl.BlockSpec((None, None, tm, ka_w), blk4),
            pl.BlockSpec((None, None, ATTN_WIDTH, tm), blk4),
            pl.BlockSpec((tm, d), row),
            pl.BlockSpec((tm, d), row),
        ],
        out_shape=[
            jax.ShapeDtypeStruct((bsz, n_t, ATTN_WIDTH, tm), BF16),
            jax.ShapeDtypeStruct((bsz, n_t, tm, ka_w), BF16),
            jax.ShapeDtypeStruct((bsz, n_t, ATTN_WIDTH, tm), BF16),
            jax.ShapeDtypeStruct((t, d), BF16),
            jax.ShapeDtypeStruct((t, d), BF16),
        ],
        scratch_shapes=[pltpu.VMEM((POOL_HALO + tm, POOL_WIDTH), F32)],
        compiler_params=_params(),
        name="inproj",
    )(x2, wq_t, wk_aug, wv_t, wu, wg, kbias, b_gate, w_mix, pool_scale, w_pool_br)


def _attn_kernel(lvec_ref, sg_ref, qt_ref, ka_ref, vt_ref, o_ref, m_scr, acc_scr, za_scr, zb_scr,
                 *, tq, lam_init):
    qi = pl.program_id(2)
    hd = DIFF_HEAD_DIM
    qt = qt_ref[...]
    sub = lax.broadcasted_iota(I32, (hd, tq), 0)
    ones_rows = jnp.where(sub < N_BIAS_PIECES, 1.0, 0.0).astype(BF16)
    qa = (jnp.concatenate([qt[0:hd], ones_rows], axis=0), jnp.concatenate([qt[hd:2 * hd], ones_rows], axis=0))

    m_scr[...] = jnp.full(m_scr.shape, -jnp.inf, F32)
    acc_scr[...] = jnp.zeros(acc_scr.shape, F32)
    sum_rows = jnp.ones((SUM_ROWS, tq), BF16)

    def scores(ki, z_ref):
        for m in range(2):
            ka = ka_ref[ki, :, m * HEAD_WIDTH:(m + 1) * HEAD_WIDTH]
            z_ref[m] = jnp.dot(ka, qa[m], preferred_element_type=F32)

    def consume(ki, z_ref, masked):
        vt = jnp.concatenate([vt_ref[ki], sum_rows], axis=0)
        for m in range(2):
            z = z_ref[m]
            if masked:
                r = lax.broadcasted_iota(I32, z.shape, 0)
                c = lax.broadcasted_iota(I32, z.shape, 1)
                z = jnp.where(r <= c, z, -jnp.inf)
            m_old = m_scr[m]
            m_new = jnp.maximum(m_old, jnp.max(z, axis=0, keepdims=True))
            alpha = jnp.exp2(m_old - m_new)
            p = jnp.exp2(z - m_new).astype(BF16)
            acc_scr[m] = alpha * acc_scr[m] + jnp.dot(vt, p, preferred_element_type=F32)
            m_scr[m] = m_new

    scores(0, za_scr)

    def pair(j, carry):
        ki = 2 * j
        scores(ki + 1, zb_scr)
        consume(ki, za_scr, False)
        scores(ki + 2, za_scr)
        consume(ki + 1, zb_scr, False)
        return carry

    lax.fori_loop(0, qi // 2, pair, 0)

    @pl.when(qi % 2 == 0)
    def _():
        consume(qi, za_scr, True)

    @pl.when(qi % 2 == 1)
    def _():
        scores(qi, zb_scr)
        consume(qi - 1, za_scr, False)
        consume(qi, zb_scr, True)

    lv = lvec_ref[...]
    lam = (jnp.exp(jnp.sum(lv[0:1] * lv[1:2], axis=-1, keepdims=True))
           - jnp.exp(jnp.sum(lv[2:3] * lv[3:4], axis=-1, keepdims=True)) + lam_init)
    hw = HEAD_WIDTH
    o = (acc_scr[0, 0:hw] / acc_scr[0, hw:hw + 1]
         - lam * (acc_scr[1, 0:hw] / acc_scr[1, hw:hw + 1]))
    ms = jnp.mean(o * o, axis=0, keepdims=True)
    o = o * lax.rsqrt(ms + LN_EPS) * sg_ref[...] * (1.0 - lam_init)
    o_ref[...] = o.T.astype(BF16)


def _attention(qt, kaug, vt, lvec, subln_g_col, lam_init):
    b, n_t, _, tq = qt.shape
    kern = functools.partial(_attn_kernel, tq=tq, lam_init=lam_init)
    return pl.pallas_call(
        kern,
        grid=(b, N_DIFF_HEADS, n_t),
        in_specs=[
            pl.BlockSpec(lvec.shape, lambda bi, h, i: (0, 0)),
            pl.BlockSpec((HEAD_WIDTH, 1), lambda bi, h, i: (0, 0)),
            pl.BlockSpec((None, None, HEAD_WIDTH, tq), lambda bi, h, i: (bi, i, h, 0)),
            pl.BlockSpec((None, n_t, tq, 2 * HEAD_WIDTH), lambda bi, h, i: (bi, 0, 0, h)),
            pl.BlockSpec((None, n_t, HEAD_WIDTH, tq), lambda bi, h, i: (bi, 0, h, 0)),
        ],
        out_specs=pl.BlockSpec((None, tq, HEAD_WIDTH), lambda bi, h, i: (bi, i, h)),
        out_shape=jax.ShapeDtypeStruct((b, n_t * tq, ATTN_WIDTH), BF16),
        scratch_shapes=[
            pltpu.VMEM((2, 1, tq), F32),
            pltpu.VMEM((2, HEAD_WIDTH + SUM_ROWS, tq), F32),
            pltpu.VMEM((2, tq, tq), F32),
            pltpu.VMEM((2, tq, tq), F32),
        ],
        compiler_params=_params(3),
        name="diffattn",
    )(lvec, subln_g_col, qt, kaug, vt)


def _mix_kernel(o_ref, ga_ref, gp_ref, x_ref, wabr_ref, wout_ref, g1_ref, b1_ref, wrh_ref, wrl_ref, br_ref,
                x1_ref, x1t_ref, topi_ref, rank_ref, gate_ref, cnt_ref, carry, *, tm, alpha):
    i = pl.program_id(0)

    @pl.when(i == 0)
    def _():
        carry[...] = jnp.zeros(carry.shape, F32)

    a_branch = jnp.dot(o_ref[...], wabr_ref[...], preferred_element_type=F32)
    mixed = ga_ref[...].astype(F32) * a_branch + gp_ref[...].astype(F32)
    y = jnp.dot(mixed.astype(BF16), wout_ref[...], preferred_element_type=F32)
    x1 = _layer_norm(alpha * x_ref[...] + y, g1_ref[...], b1_ref[...])
    x1_ref[...] = x1
    _store_token_tiles(x1t_ref, x1)

    x_hi = x1.astype(BF16)
    x_lo = (x1 - x_hi.astype(F32)).astype(BF16)
    logits = (jnp.dot(x_hi, wrh_ref[...], preferred_element_type=F32)
              + jnp.dot(x_lo, wrh_ref[...], preferred_element_type=F32)
              + jnp.dot(x_hi, wrl_ref[...], preferred_element_type=F32)) + br_ref[...]
    e_iota = lax.broadcasted_iota(I32, logits.shape, 1).astype(F32)
    vals, idxs = [], []
    rem = logits
    for _ in range(TOP_K):
        m = jnp.max(rem, axis=-1, keepdims=True)
        idx = jnp.min(jnp.where(rem == m, e_iota, float(N_EXPERTS)), axis=-1, keepdims=True)
        vals.append(m)
        idxs.append(idx)
        rem = jnp.where(e_iota == idx, -jnp.inf, rem)
    exps = [jnp.exp(vk - vals[0]) for vk in vals]
    den = exps[0] + exps[1] + exps[2] + exps[3]

    sel = jnp.zeros(logits.shape, F32)
    for idx in idxs:
        sel = sel + jnp.where(e_iota == idx, 1.0, 0.0)
    r = lax.broadcasted_iota(I32, (tm, tm), 0)
    c = lax.broadcasted_iota(I32, (tm, tm), 1)
    tri = jnp.where(c < r, 1.0, 0.0).astype(BF16)
    prefix = jnp.dot(tri, sel.astype(BF16), preferred_element_type=F32) + carry[...]
    carry[...] = carry[...] + jnp.sum(sel, axis=0, keepdims=True)
    cnt_ref[...] = carry[...]

    lane = lax.broadcasted_iota(I32, (tm, LANES), 1)
    topi = jnp.zeros((tm, LANES), F32)
    rank = jnp.zeros((tm, LANES), F32)
    gate = jnp.zeros((tm, LANES), F32)
    for kk in range(TOP_K):
        rk = jnp.sum(jnp.where(e_iota == idxs[kk], prefix, 0.0), axis=-1, keepdims=True)
        topi = jnp.where(lane == kk, idxs[kk], topi)
        rank = jnp.where(lane == kk, rk, rank)
        gate = jnp.where(lane == kk, exps[kk] / den, gate)
    topi_ref[...] = topi.astype(I32)
    rank_ref[...] = rank.astype(I32)
    gate_ref[...] = gate


def _mix(o2, ga, gp, x2, w_attn_br, w_out, ln_g, ln_b, w_router_hi, w_router_lo, b_router, alpha):
    t, d = x2.shape
    tm = TOKEN_TILE
    row = lambda i: (i, 0)
    full2 = lambda i: (0, 0)
    kern = functools.partial(_mix_kernel, tm=tm, alpha=alpha)
    return pl.pallas_call(
        kern,
        grid=(t // tm,),
        in_specs=[
            pl.BlockSpec((tm, ATTN_WIDTH), row),
            pl.BlockSpec((tm, d), row),
            pl.BlockSpec((tm, d), row),
            pl.BlockSpec((tm, d), row),
            pl.BlockSpec((ATTN_WIDTH, d), full2),
            pl.BlockSpec((d, d), full2),
            pl.BlockSpec((1, d), full2),
            pl.BlockSpec((1, d), full2),
            pl.BlockSpec((d, N_EXPERTS), full2),
            pl.BlockSpec((d, N_EXPERTS), full2),
            pl.BlockSpec((1, N_EXPERTS), full2),
        ],
        out_specs=[
            pl.BlockSpec((tm, d), row),
            pl.BlockSpec((tm * ROW_TILE, LANES), row),
            pl.BlockSpec((tm, LANES), row),
            pl.BlockSpec((tm, LANES), row),
            pl.BlockSpec((tm, LANES), row),
            pl.BlockSpec((1, N_EXPERTS), full2),
        ],
        out_shape=[
            jax.ShapeDtypeStruct((t, d), F32),
            jax.ShapeDtypeStruct((t * ROW_TILE, LANES), F32),
            jax.ShapeDtypeStruct((t, LANES), I32),
            jax.ShapeDtypeStruct((t, LANES), I32),
            jax.ShapeDtypeStruct((t, LANES), F32),
            jax.ShapeDtypeStruct((1, N_EXPERTS), F32),
        ],
        scratch_shapes=[pltpu.VMEM((1, N_EXPERTS), F32)],
        compiler_params=_params(),
        name="mix_ln1_router",
    )(o2, ga, gp, x2, w_attn_br, w_out, ln_g, ln_b, w_router_hi, w_router_lo, b_router)


def _store_token_tiles(ref, val):
    rows = val.shape[0]
    for s in range(ROW_TILE):
        ref[pl.ds(s, rows, stride=ROW_TILE), :] = val[:, s * LANES:(s + 1) * LANES]


def _load_token_tiles(ref, rows):
    return jnp.concatenate([ref[pl.ds(s, rows, stride=ROW_TILE), :] for s in range(ROW_TILE)], axis=-1)


def _tile_copy(src_ref, src_row8, dst_ref, dst_row8, sem):
    return pltpu.make_async_copy(src_ref.at[pl.ds(pl.multiple_of(src_row8, ROW_TILE), ROW_TILE), :],
                                 dst_ref.at[pl.ds(pl.multiple_of(dst_row8, ROW_TILE), ROW_TILE), :], sem)


def _dispatch_kernel(dest_ref, x1t_ref, x1_ref, p_ref, wg_ref, wp_ref, xs_hbm, ple_ref, sem, *, tm):
    def body(t, carry):
        for kk in range(TOP_K):
            _tile_copy(x1t_ref, t * ROW_TILE, xs_hbm, dest_ref[TOP_K * t + kk], sem).start(priority=kk % 2)
        return carry

    lax.fori_loop(0, tm, body, 0, unroll=ISSUE_UNROLL)
    ple_ref[...] = (_sigmoid(jnp.dot(x1_ref[...].astype(BF16), wg_ref[...], preferred_element_type=F32))
                    * jnp.dot(p_ref[...].astype(BF16), wp_ref[...], preferred_element_type=F32))
    for _ in range(TOP_K):
        pltpu.make_async_copy(x1t_ref, xs_hbm.at[pl.ds(0, tm * ROW_TILE), :], sem).wait()


def _dispatch(dest8, x1t, x1, p2, w_ple_gate, w_ple_proj, n_slots):
    tm = TOKEN_TILE
    t, d = x1.shape
    ple_dim = p2.shape[1]
    row = lambda i: (i, 0)
    full2 = lambda i: (0, 0)
    kern = functools.partial(_dispatch_kernel, tm=tm)
    return pl.pallas_call(
        kern,
        grid=(t // tm,),
        in_specs=[
            pl.BlockSpec((TOP_K * tm,), lambda i: (i,), memory_space=pltpu.SMEM),
            pl.BlockSpec((tm * ROW_TILE, LANES), row),
            pl.BlockSpec((tm, d), row),
            pl.BlockSpec((tm, ple_dim), row),
            pl.BlockSpec((d, d), full2),
            pl.BlockSpec((ple_dim, d), full2),
        ],
        out_specs=[pl.BlockSpec(memory_space=pl.ANY), pl.BlockSpec((tm, d), row)],
        out_shape=[jax.ShapeDtypeStruct((n_slots * ROW_TILE, LANES), F32), jax.ShapeDtypeStruct((t, d), F32)],
        scratch_shapes=[pltpu.SemaphoreType.DMA(())],
        compiler_params=_params(),
        name="dispatch",
    )(dest8, x1t, x1, p2, w_ple_gate, w_ple_proj)


def _expert_kernel(be_ref, bsrc_ref, bv_ref, xs_ref, wup_ref, bup_ref, wdn_ref, bdn_ref, ys_ref,
                   wup_bf, wdn_bf, *, bm, d_ff):
    b = pl.program_id(0)
    nv = bv_ref[b]

    @pl.when(jnp.logical_or(b == 0, be_ref[b] != be_ref[jnp.maximum(b - 1, 0)]))
    def _():
        wup_bf[...] = wup_ref[0].astype(BF16)
        wdn_bf[...] = wdn_ref[0].astype(BF16)

    @pl.when(nv > 0)
    def _():
        rows = lax.broadcasted_iota(I32, (bm, 1), 0)
        x = jnp.where(rows < nv, _load_token_tiles(xs_ref, bm), 0.0).astype(BF16)
        hid = jnp.dot(x, wup_bf[...], preferred_element_type=F32) + bup_ref[0]
        glu = jnp.minimum(hid[:, :d_ff], SWIGLU_LIMIT)
        lin = jnp.clip(hid[:, d_ff:], -SWIGLU_LIMIT, SWIGLU_LIMIT)
        act = glu * _sigmoid(SWIGLU_ALPHA * glu) * (lin + 1.0)
        y = jnp.dot(act.astype(BF16), wdn_bf[...], preferred_element_type=F32) + bdn_ref[0]
        _store_token_tiles(ys_ref, y)


def _experts(be, bsrc, bv, xs, w_up, b_up, w_down, b_down):
    bm = EXPERT_BLOCK
    n_slots = xs.shape[0] // ROW_TILE
    d_ff, d = w_down.shape[1:]
    kern = functools.partial(_expert_kernel, bm=bm, d_ff=d_ff)
    blk = lambda b, be_r, bsrc_r, bv_r: (bsrc_r[b], 0)
    wsel = lambda b, be_r, bsrc_r, bv_r: (be_r[b], 0, 0)
    grid_spec = pltpu.PrefetchScalarGridSpec(
        num_scalar_prefetch=3,
        grid=(n_slots // bm,),
        in_specs=[
            pl.BlockSpec((bm * ROW_TILE, LANES), blk),
            pl.BlockSpec((1, d, 2 * d_ff), wsel),
            pl.BlockSpec((1, 1, 2 * d_ff), wsel),
            pl.BlockSpec((1, d_ff, d), wsel),
            pl.BlockSpec((1, 1, d), wsel),
        ],
        out_specs=pl.BlockSpec((bm * ROW_TILE, LANES), blk),
        scratch_shapes=[pltpu.VMEM((d, 2 * d_ff), BF16), pltpu.VMEM((d_ff, d), BF16)],
    )
    return pl.pallas_call(
        kern,
        grid_spec=grid_spec,
        out_shape=jax.ShapeDtypeStruct((n_slots * ROW_TILE, LANES), F32),
        compiler_params=_params(),
        name="experts",
    )(be, bsrc, bv, xs, w_up, b_up, w_down, b_down)


def _final_kernel(dest_ref, dnext_ref, x1_ref, ple_ref, gate_ref, g2_ref, b2_ref, ys_hbm,
                  o_ref, ybuf, sem, *, tm, alpha):
    i = pl.program_id(0)
    slot = i % 2

    def gather(ids_ref, buf_slot):
        def body(t, carry):
            for kk in range(TOP_K):
                _tile_copy(ys_hbm, ids_ref[TOP_K * t + kk], ybuf.at[buf_slot, kk], t * ROW_TILE,
                           sem.at[buf_slot]).start(priority=kk % 2)
            return carry

        lax.fori_loop(0, tm, body, 0, unroll=ISSUE_UNROLL)

    @pl.when(i == 0)
    def _():
        gather(dest_ref, 0)

    @pl.when(i + 1 < pl.num_programs(0))
    def _():
        gather(dnext_ref, 1 - slot)

    for kk in range(TOP_K):
        pltpu.make_async_copy(ys_hbm.at[pl.ds(0, tm * ROW_TILE), :], ybuf.at[slot, kk], sem.at[slot]).wait()
    gate = gate_ref[...]
    ffn = gate[:, 0:1] * _load_token_tiles(ybuf.at[slot, 0], tm)
    for kk in range(1, TOP_K):
        ffn = ffn + gate[:, kk:kk + 1] * _load_token_tiles(ybuf.at[slot, kk], tm)
    o_ref[...] = _layer_norm(alpha * x1_ref[...] + ffn + ple_ref[...], g2_ref[...], b2_ref[...])


def _final(dest, x1, ple, gate, ln_g, ln_b, ys, alpha):
    t, d = x1.shape
    tm = COMBINE_TILE
    n_tiles = t // tm
    row = lambda i: (i, 0)
    full2 = lambda i: (0, 0)
    kern = functools.partial(_final_kernel, tm=tm, alpha=alpha)
    return pl.pallas_call(
        kern,
        grid=(n_tiles,),
        in_specs=[
            pl.BlockSpec((TOP_K * tm,), lambda i: (i,), memory_space=pltpu.SMEM),
            pl.BlockSpec((TOP_K * tm,), lambda i: (jnp.minimum(i + 1, n_tiles - 1),), memory_space=pltpu.SMEM),
            pl.BlockSpec((tm, d), row),
            pl.BlockSpec((tm, d), row),
            pl.BlockSpec((tm, LANES), row),
            pl.BlockSpec((1, d), full2),
            pl.BlockSpec((1, d), full2),
            pl.BlockSpec(memory_space=pl.ANY),
        ],
        out_specs=pl.BlockSpec((tm, d), row),
        out_shape=jax.ShapeDtypeStruct((t, d), F32),
        scratch_shapes=[pltpu.VMEM((2, TOP_K, tm * ROW_TILE, LANES), F32), pltpu.SemaphoreType.DMA((2,))],
        compiler_params=_params(),
        name="combine_ln2",
    )(dest, dest, x1, ple, gate, ln_g, ln_b, ys)


def _augment_k_weight(wk):
    d = wk.shape[0]
    w = wk.reshape(d, N_DIFF_HEADS * 2, DIFF_HEAD_DIM)
    w = jnp.pad(w, ((0, 0), (0, 0), (0, HEAD_WIDTH - DIFF_HEAD_DIM)))
    return w.reshape(d, N_DIFF_HEADS * 2 * HEAD_WIDTH)


def _alibi_key_bias(seq):
    slopes = np.asarray([2.0 ** (-8.0 * (h + 1) / N_DIFF_HEADS) for h in range(N_DIFF_HEADS)], np.float32)
    b = slopes[None, :] * np.arange(seq, dtype=np.float32)[:, None] * np.float32(LOG2E)
    pieces = []
    rem = b
    for _ in range(N_BIAS_PIECES):
        piece = (rem.view(np.uint32) & np.uint32(0xFFFF0000)).view(np.float32)
        pieces.append(piece)
        rem = rem - piece
    pb = jnp.asarray(np.stack(pieces, axis=-1)).astype(BF16)
    pb = jnp.broadcast_to(pb[:, :, None, :], (seq, N_DIFF_HEADS, 2, N_BIAS_PIECES))
    pb = jnp.pad(pb, ((0, 0), (0, 0), (0, 0), (DIFF_HEAD_DIM, HEAD_WIDTH - DIFF_HEAD_DIM - N_BIAS_PIECES)))
    return pb.reshape(seq, N_DIFF_HEADS * 2 * HEAD_WIDTH)


def _split_bf16(w):
    hi = lax.bitcast_convert_type(lax.bitcast_convert_type(w, jnp.uint32) & jnp.uint32(0xFFFF0000), F32)
    return hi.astype(BF16), (w - hi).astype(BF16)


def _routing_tables(counts, topi, rank, n_blocks):
    bm = EXPERT_BLOCK
    n_blk_e = (counts + bm - 1) // bm
    cum_blk = jnp.cumsum(n_blk_e)
    blk_start = cum_blk - n_blk_e
    n_used = cum_blk[-1]
    experts = jnp.arange(N_EXPERTS, dtype=I32)
    slot_start = jnp.sum(jnp.where(topi[..., None] == experts, blk_start * bm, 0), axis=-1)
    dest = (slot_start + rank).reshape(-1).astype(I32)
    bidx = jnp.arange(n_blocks, dtype=I32)
    bsrc = jnp.minimum(bidx, n_used - 1).astype(I32)
    be = jnp.minimum(jnp.sum((cum_blk[None, :] <= bsrc[:, None]).astype(I32), axis=-1), N_EXPERTS - 1)
    valid = jnp.clip(counts[be] - (bsrc - blk_start[be]) * bm, 0, bm)
    bv = jnp.where(bidx < n_used, valid, 0).astype(I32)
    return dest, be, bsrc, bv


def kernel(x, p, w_in, b_gate, lambda_q1, lambda_k1, lambda_q2, lambda_k2, subln_g, w_attn_br,
           w_pool_mix, pool_scale, w_pool_br, w_out, ln1_g, ln1_b, w_router, b_router, w_up, b_up,
           w_down, b_down, w_ple_gate, w_ple_proj, ln2_g, ln2_b):
    bsz, seq, d = x.shape
    depth = w_in.shape[0]
    t = bsz * seq
    assert seq % ATTN_TILE == 0 and seq % TOKEN_TILE == 0 and t % EXPERT_BLOCK == 0
    alpha = (2 * depth) ** 0.25
    n_blocks = (t * TOP_K) // EXPERT_BLOCK + N_EXPERTS
    n_slots = n_blocks * EXPERT_BLOCK

    x2 = x.reshape(t, d)
    for i in range(depth):
        lam_init = 0.8 - 0.6 * math.exp(-0.3 * i)
        wq, wk, wv, wu, wg = jnp.split(w_in[i].astype(BF16), IN_SPLITS, axis=-1)
        qt, kaug, vt, ga, gp = _inproj(x2, wq.T, _augment_k_weight(wk), wv.T, wu, wg, _alibi_key_bias(seq),
                                       b_gate[i][None], w_pool_mix[i].astype(BF16), pool_scale[i][None],
                                       w_pool_br[i].astype(BF16), bsz, seq)
        lvec = jnp.stack([lambda_q1[i], lambda_k1[i], lambda_q2[i], lambda_k2[i]]).astype(F32)
        o = _attention(qt, kaug, vt, lvec, subln_g[i][:, None], lam_init)
        x1, x1t, topi, rank, gate, cnt = _mix(o.reshape(t, ATTN_WIDTH), ga, gp, x2,
                                         w_attn_br[i].astype(BF16), w_out[i].astype(BF16),
                                         ln1_g[i][None], ln1_b[i][None], *_split_bf16(w_router[i]),
                                         b_router[i][None], alpha)
        counts = cnt[0].astype(I32)
        dest, be, bsrc, bv = _routing_tables(counts, topi[:, :TOP_K], rank[:, :TOP_K], n_blocks)
        dest8 = dest * ROW_TILE
        xs, ple = _dispatch(dest8, x1t, x1, p[i].reshape(t, -1), w_ple_gate[i].astype(BF16),
                            w_ple_proj[i].astype(BF16), n_slots)
        ys = _experts(be, bsrc, bv, xs, w_up[i], b_up[i][:, None, :], w_down[i], b_down[i][:, None, :])
        x2 = _final(dest8, x1, ple, gate, ln2_g[i][None], ln2_b[i][None], ys, alpha)
    return x2.reshape(bsz, seq, d)
```

```python
import functools
import math

import jax
import jax.numpy as jnp
import numpy as np
from jax import lax
from jax.experimental import pallas as pl
from jax.experimental.pallas import tpu as pltpu

F32 = jnp.float32
BF16 = jnp.bfloat16
I32 = jnp.int32

N_DIFF_HEADS = 4
DIFF_HEAD_DIM = 64
HEAD_WIDTH = 2 * DIFF_HEAD_DIM
ATTN_WIDTH = N_DIFF_HEADS * HEAD_WIDTH
POOL_WINDOWS = (2, 4, 8, 16)
POOL_GROUP_DIM = 128
POOL_WIDTH = len(POOL_WINDOWS) * POOL_GROUP_DIM
POOL_HALO = 16
N_EXPERTS = 32
TOP_K = 4
SWIGLU_LIMIT = 7.0
SWIGLU_ALPHA = 1.702
LN_EPS = 1e-5
LANES = 128
ROW_TILE = 8
LOG2E = math.log2(math.e)
N_BIAS_PIECES = 3
IN_SPLITS = (ATTN_WIDTH, 2 * ATTN_WIDTH, 3 * ATTN_WIDTH, 3 * ATTN_WIDTH + POOL_WIDTH)

TOKEN_TILE = 512
ATTN_TILE = 512
SUM_ROWS = 16
EXPERT_BLOCK = 512
COMBINE_TILE = 512
ISSUE_UNROLL = 4
VMEM_LIMIT = 56 * 1024 * 1024


def _sigmoid(z):
    return 1.0 / (1.0 + jnp.exp(-z))


def _layer_norm(z, g, b):
    mu = jnp.mean(z, axis=-1, keepdims=True)
    zc = z - mu
    var = jnp.mean(zc * zc, axis=-1, keepdims=True)
    return zc * lax.rsqrt(var + LN_EPS) * g + b


def _params(n_axes=1):
    return pltpu.CompilerParams(dimension_semantics=("arbitrary",) * n_axes,
                                vmem_limit_bytes=VMEM_LIMIT)


def _inproj_kernel(x_ref, wq_ref, wk_ref, wv_ref, wu_ref, wg_ref, kb_ref, bg_ref, wmix_ref, pscale_ref,
                   wpbr_ref, qt_ref, ka_ref, vt_ref, ga_ref, gp_ref, ubuf, *, tiles_per_seq, tm, d_model):
    j = pl.program_id(0) % tiles_per_seq
    xb = x_ref[...].astype(BF16)

    def proj(w_ref, lo=None, hi=None):
        w = w_ref[...] if lo is None else w_ref[:, lo:hi]
        return jnp.dot(xb, w, preferred_element_type=F32)

    def proj_t(wt_ref):
        return lax.dot_general(wt_ref[...], xb, (((1,), (1,)), ((), ())), preferred_element_type=F32)

    qt_ref[...] = (proj_t(wq_ref) * (DIFF_HEAD_DIM ** -0.5 * LOG2E)).astype(BF16)
    ka_ref[...] = (proj(wk_ref) + kb_ref[...].astype(F32)).astype(BF16)
    vt_ref[...] = proj_t(wv_ref).astype(BF16)
    u = proj(wu_ref)

    @pl.when(j == 0)
    def _():
        ubuf[0:POOL_HALO, :] = jnp.zeros((POOL_HALO, POOL_WIDTH), F32)

    ubuf[POOL_HALO:, :] = u
    pos = j * tm + lax.broadcasted_iota(I32, (tm, 1), 0)
    parts = []
    for gi, w in enumerate(POOL_WINDOWS):
        lo, hi = gi * POOL_GROUP_DIM, (gi + 1) * POOL_GROUP_DIM
        a = ubuf[:, lo:hi]
        s = a
        sh = 1
        while sh < w:
            s = s + pltpu.roll(s, sh, axis=0)
            sh *= 2
        cnt = jnp.minimum(pos + 1, w).astype(F32)
        pooled = s[POOL_HALO:, :] / cnt - a[POOL_HALO:, :]
        mixed = jnp.dot(pooled.astype(BF16), wmix_ref[gi], preferred_element_type=F32)
        parts.append(mixed * pscale_ref[:, lo:hi])
    ubuf[0:POOL_HALO, :] = u[tm - POOL_HALO:, :]
    p_branch = jnp.dot(jnp.concatenate(parts, axis=-1).astype(BF16), wpbr_ref[...],
                       preferred_element_type=F32)

    g_a = _sigmoid(proj(wg_ref, 0, d_model) + bg_ref[:, 0:d_model])
    ga_ref[...] = g_a.astype(BF16)
    g_p = _sigmoid(proj(wg_ref, d_model, 2 * d_model) + bg_ref[:, d_model:2 * d_model])
    gp_ref[...] = (g_p * p_branch).astype(BF16)


def _inproj(x2, wq_t, wk_aug, wv_t, wu, wg, kbias, b_gate, w_mix, pool_scale, w_pool_br, bsz, seq):
    t, d = x2.shape
    tm = ATTN_TILE
    n_t = seq // tm
    row = lambda i: (i, 0)
    full2 = lambda i: (0, 0)
    blk4 = lambda i: (i // n_t, i % n_t, 0, 0)
    kern = functools.partial(_inproj_kernel, tiles_per_seq=n_t, tm=tm, d_model=d)
    ka_w = wk_aug.shape[1]
    return pl.pallas_call(
        kern,
        grid=(t // tm,),
        in_specs=[
            pl.BlockSpec((tm, d), row),
            pl.BlockSpec(wq_t.shape, full2),
            pl.BlockSpec(wk_aug.shape, full2),
            pl.BlockSpec(wv_t.shape, full2),
            pl.BlockSpec(wu.shape, full2),
            pl.BlockSpec(wg.shape, full2),
            pl.BlockSpec((tm, ka_w), lambda i: (i % n_t, 0)),
            pl.BlockSpec((1, 2 * d), full2),
            pl.BlockSpec(w_mix.shape, lambda i: (0, 0, 0)),
            pl.BlockSpec((1, POOL_WIDTH), full2),
            pl.BlockSpec((POOL_WIDTH, d), full2),
        ],
        out_specs=[
            pl.BlockSpec((None, None, ATTN_WIDTH, tm), blk4),
            pl.BlockSpec((None, None, tm, ka_w), blk4),
            pl.BlockSpec((None, None, ATTN_WIDTH, tm), blk4),
            pl.BlockSpec((tm, d), row),
            pl.BlockSpec((tm, d), row),
        ],
        out_shape=[
            jax.ShapeDtypeStruct((bsz, n_t, ATTN_WIDTH, tm), BF16),
            jax.ShapeDtypeStruct((bsz, n_t, tm, ka_w), BF16),
            jax.ShapeDtypeStruct((bsz, n_t, ATTN_WIDTH, tm), BF16),
            jax.ShapeDtypeStruct((t, d), BF16),
            jax.ShapeDtypeStruct((t, d), BF16),
        ],
        scratch_shapes=[pltpu.VMEM((POOL_HALO + tm, POOL_WIDTH), F32)],
        compiler_params=_params(),
        name="inproj",
    )(x2, wq_t, wk_aug, wv_t, wu, wg, kbias, b_gate, w_mix, pool_scale, w_pool_br)


def _attn_kernel(lvec_ref, sg_ref, qt_ref, ka_ref, vt_ref, o_ref, m_scr, acc_scr, za_scr, zb_scr,
                 *, tq, lam_init):
    qi = pl.program_id(2)
    hd = DIFF_HEAD_DIM
    qt = qt_ref[...]
    sub = lax.broadcasted_iota(I32, (hd, tq), 0)
    ones_rows = jnp.where(sub < N_BIAS_PIECES, 1.0, 0.0).astype(BF16)
    qa = (jnp.concatenate([qt[0:hd], ones_rows], axis=0), jnp.concatenate([qt[hd:2 * hd], ones_rows], axis=0))

    m_scr[...] = jnp.full(m_scr.shape, -jnp.inf, F32)
    acc_scr[...] = jnp.zeros(acc_scr.shape, F32)
    sum_rows = jnp.ones((SUM_ROWS, tq), BF16)

    def scores(ki, z_ref):
        for m in range(2):
            ka = ka_ref[ki, :, m * HEAD_WIDTH:(m + 1) * HEAD_WIDTH]
            z_ref[m] = jnp.dot(ka, qa[m], preferred_element_type=F32)

    def consume(ki, z_ref, masked):
        vt = jnp.concatenate([vt_ref[ki], sum_rows], axis=0)
        for m in range(2):
            z = z_ref[m]
            if masked:
                r = lax.broadcasted_iota(I32, z.shape, 0)
                c = lax.broadcasted_iota(I32, z.shape, 1)
                z = jnp.where(r <= c, z, -jnp.inf)
            m_old = m_scr[m]
            m_new = jnp.maximum(m_old, jnp.max(z, axis=0, keepdims=True))
            alpha = jnp.exp2(m_old - m_new)
            p = jnp.exp2(z - m_new).astype(BF16)
            acc_scr[m] = alpha * acc_scr[m] + jnp.dot(vt, p, preferred_element_type=F32)
            m_scr[m] = m_new

    scores(0, za_scr)

    def pair(j, carry):
        ki = 2 * j
        scores(ki + 1, zb_scr)
        consume(ki, za_scr, False)
        scores(ki + 2, za_scr)
        consume(ki + 1, zb_scr, False)
        return carry

    lax.fori_loop(0, qi // 2, pair, 0)

    @pl.when(qi % 2 == 0)
    def _():
        consume(qi, za_scr, True)

    @pl.when(qi % 2 == 1)
    def _():
        scores(qi, zb_scr)
        consume(qi - 1, za_scr, False)
        consume(qi, zb_scr, True)

    lv = lvec_ref[...]
    lam = (jnp.exp(jnp.sum(lv[0:1] * lv[1:2], axis=-1, keepdims=True))
           - jnp.exp(jnp.sum(lv[2:3] * lv[3:4], axis=-1, keepdims=True)) + lam_init)
    hw = HEAD_WIDTH
    o = (acc_scr[0, 0:hw] / acc_scr[0, hw:hw + 1]
         - lam * (acc_scr[1, 0:hw] / acc_scr[1, hw:hw + 1]))
    ms = jnp.mean(o * o, axis=0, keepdims=True)
    o = o * lax.rsqrt(ms + LN_EPS) * sg_ref[...] * (1.0 - lam_init)
    o_ref[...] = o.T.astype(BF16)


def _attention(qt, kaug, vt, lvec, subln_g_col, lam_init):
    b, n_t, _, tq = qt.shape
    kern = functools.partial(_attn_kernel, tq=tq, lam_init=lam_init)
    return pl.pallas_call(
        kern,
        grid=(b, N_DIFF_HEADS, n_t),
        in_specs=[
            pl.BlockSpec(lvec.shape, lambda bi, h, i: (0, 0)),
            pl.BlockSpec((HEAD_WIDTH, 1), lambda bi, h, i: (0, 0)),
            pl.BlockSpec((None, None, HEAD_WIDTH, tq), lambda bi, h, i: (bi, i, h, 0)),
            pl.BlockSpec((None, n_t, tq, 2 * HEAD_WIDTH), lambda bi, h, i: (bi, 0, 0, h)),
            pl.BlockSpec((None, n_t, HEAD_WIDTH, tq), lambda bi, h, i: (bi, 0, h, 0)),
        ],
        out_specs=pl.BlockSpec((None, tq, HEAD_WIDTH), lambda bi, h, i: (bi, i, h)),
        out_shape=jax.ShapeDtypeStruct((b, n_t * tq, ATTN_WIDTH), BF16),
        scratch_shapes=[
            pltpu.VMEM((2, 1, tq), F32),
            pltpu.VMEM((2, HEAD_WIDTH + SUM_ROWS, tq), F32),
            pltpu.VMEM((2, tq, tq), F32),
            pltpu.VMEM((2, tq, tq), F32),
        ],
        compiler_params=_params(3),
        name="diffattn",
    )(lvec, subln_g_col, qt, kaug, vt)


def _mix_kernel(o_ref, ga_ref, gp_ref, x_ref, wabr_ref, wout_ref, g1_ref, b1_ref, wrh_ref, wrl_ref, br_ref,
                x1_ref, x1t_ref, topi_ref, rank_ref, gate_ref, cnt_ref, carry, *, tm, alpha):
    i = pl.program_id(0)

    @pl.when(i == 0)
    def _():
        carry[...] = jnp.zeros(carry.shape, F32)

    a_branch = jnp.dot(o_ref[...], wabr_ref[...], preferred_element_type=F32)
    mixed = ga_ref[...].astype(F32) * a_branch + gp_ref[...].astype(F32)
    y = jnp.dot(mixed.astype(BF16), wout_ref[...], preferred_element_type=F32)
    x1 = _layer_norm(alpha * x_ref[...] + y, g1_ref[...], b1_ref[...])
    x1_ref[...] = x1
    _store_token_tiles(x1t_ref, x1)

    x_hi = x1.astype(BF16)
    x_lo = (x1 - x_hi.astype(F32)).astype(BF16)
    logits = (jnp.dot(x_hi, wrh_ref[...], preferred_element_type=F32)
              + jnp.dot(x_lo, wrh_ref[...], preferred_element_type=F32)
              + jnp.dot(x_hi, wrl_ref[...], preferred_element_type=F32)) + br_ref[...]
    e_iota = lax.broadcasted_iota(I32, logits.shape, 1).astype(F32)
    vals, idxs = [], []
    rem = logits
    for _ in range(TOP_K):
        m = jnp.max(rem, axis=-1, keepdims=True)
        idx = jnp.min(jnp.where(rem == m, e_iota, float(N_EXPERTS)), axis=-1, keepdims=True)
        vals.append(m)
        idxs.append(idx)
        rem = jnp.where(e_iota == idx, -jnp.inf, rem)
    exps = [jnp.exp(vk - vals[0]) for vk in vals]
    den = exps[0] + exps[1] + exps[2] + exps[3]

    sel = jnp.zeros(logits.shape, F32)
    for idx in idxs:
        sel = sel + jnp.where(e_iota == idx, 1.0, 0.0)
    r = lax.broadcasted_iota(I32, (tm, tm), 0)
    c = lax.broadcasted_iota(I32, (tm, tm), 1)
    tri = jnp.where(c < r, 1.0, 0.0).astype(BF16)
    prefix = jnp.dot(tri, sel.astype(BF16), preferred_element_type=F32) + carry[...]
    carry[...] = carry[...] + jnp.sum(sel, axis=0, keepdims=True)
    cnt_ref[...] = carry[...]

    lane = lax.broadcasted_iota(I32, (tm, TOP_K), 1)
    topi = jnp.zeros((tm, TOP_K), F32)
    rank = jnp.zeros((tm, TOP_K), F32)
    gate = jnp.zeros((tm, TOP_K), F32)
    for kk in range(TOP_K):
        rk = jnp.sum(jnp.where(e_iota == idxs[kk], prefix, 0.0), axis=-1, keepdims=True)
        topi = jnp.where(lane == kk, idxs[kk], topi)
        rank = jnp.where(lane == kk, rk, rank)
        gate = jnp.where(lane == kk, exps[kk] / den, gate)
    topi_ref[...] = topi.astype(I32)
    rank_ref[...] = rank.astype(I32)
    gate_ref[...] = gate


def _mix(o2, ga, gp, x2, w_attn_br, w_out, ln_g, ln_b, w_router_hi, w_router_lo, b_router, alpha):
    t, d = x2.shape
    tm = TOKEN_TILE
    row = lambda i: (i, 0)
    full2 = lambda i: (0, 0)
    kern = functools.partial(_mix_kernel, tm=tm, alpha=alpha)
    return pl.pallas_call(
        kern,
        grid=(t // tm,),
        in_specs=[
            pl.BlockSpec((tm, ATTN_WIDTH), row),
            pl.BlockSpec((tm, d), row),
            pl.BlockSpec((tm, d), row),
            pl.BlockSpec((tm, d), row),
            pl.BlockSpec((ATTN_WIDTH, d), full2),
            pl.BlockSpec((d, d), full2),
            pl.BlockSpec((1, d), full2),
            pl.BlockSpec((1, d), full2),
            pl.BlockSpec((d, N_EXPERTS), full2),
            pl.BlockSpec((d, N_EXPERTS), full2),
            pl.BlockSpec((1, N_EXPERTS), full2),
        ],
        out_specs=[
            pl.BlockSpec((tm, d), row),
            pl.BlockSpec((tm * ROW_TILE, LANES), row),
            pl.BlockSpec((tm, TOP_K), row),
            pl.BlockSpec((tm, TOP_K), row),
            pl.BlockSpec((tm, TOP_K), row),
            pl.BlockSpec((1, N_EXPERTS), full2),
        ],
        out_shape=[
            jax.ShapeDtypeStruct((t, d), F32),
            jax.ShapeDtypeStruct((t * ROW_TILE, LANES), F32),
            jax.ShapeDtypeStruct((t, TOP_K), I32),
            jax.ShapeDtypeStruct((t, TOP_K), I32),
            jax.ShapeDtypeStruct((t, TOP_K), F32),
            jax.ShapeDtypeStruct((1, N_EXPERTS), F32),
        ],
        scratch_shapes=[pltpu.VMEM((1, N_EXPERTS), F32)],
        compiler_params=_params(),
        name="mix_ln1_router",
    )(o2, ga, gp, x2, w_attn_br, w_out, ln_g, ln_b, w_router_hi, w_router_lo, b_router)


def _store_token_tiles(ref, val):
    rows = val.shape[0]
    for s in range(ROW_TILE):
        ref[pl.ds(s, rows, stride=ROW_TILE), :] = val[:, s * LANES:(s + 1) * LANES]


def _load_token_tiles(ref, rows):
    return jnp.concatenate([ref[pl.ds(s, rows, stride=ROW_TILE), :] for s in range(ROW_TILE)], axis=-1)


def _tile_copy(src_ref, src_row8, dst_ref, dst_row8, sem):
    return pltpu.make_async_copy(src_ref.at[pl.ds(pl.multiple_of(src_row8, ROW_TILE), ROW_TILE), :],
                                 dst_ref.at[pl.ds(pl.multiple_of(dst_row8, ROW_TILE), ROW_TILE), :], sem)


def _dispatch_kernel(dest_ref, unused_ref, x1t_ref, x1_ref, p_ref, wg_ref, wp_ref, xs_hbm, ple_ref, sem, fill_sem,
                     *, tm):
    n_unused = unused_ref.shape[0]

    @pl.when(pl.program_id(0) == 0)
    def _():
        def fill(r, carry):
            _tile_copy(x1t_ref, 0, xs_hbm, unused_ref[r], fill_sem).start(priority=1)
            return carry

        lax.fori_loop(0, n_unused, fill, 0, unroll=ISSUE_UNROLL)
        for _ in range(n_unused // tm):
            pltpu.make_async_copy(x1t_ref, xs_hbm.at[pl.ds(0, tm * ROW_TILE), :], fill_sem).wait()

    def body(t, carry):
        for kk in range(TOP_K):
            _tile_copy(x1t_ref, t * ROW_TILE, xs_hbm, dest_ref[TOP_K * t + kk], sem).start(priority=kk % 2)
        return carry

    lax.fori_loop(0, tm, body, 0, unroll=ISSUE_UNROLL)
    ple_ref[...] = (_sigmoid(jnp.dot(x1_ref[...].astype(BF16), wg_ref[...], preferred_element_type=F32))
                    * jnp.dot(p_ref[...].astype(BF16), wp_ref[...], preferred_element_type=F32))
    for _ in range(TOP_K):
        pltpu.make_async_copy(x1t_ref, xs_hbm.at[pl.ds(0, tm * ROW_TILE), :], sem).wait()


def _dispatch(dest8, unused8, x1t, x1, p2, w_ple_gate, w_ple_proj, n_slots):
    tm = TOKEN_TILE
    t, d = x1.shape
    ple_dim = p2.shape[1]
    row = lambda i: (i, 0)
    full2 = lambda i: (0, 0)
    kern = functools.partial(_dispatch_kernel, tm=tm)
    return pl.pallas_call(
        kern,
        grid=(t // tm,),
        in_specs=[
            pl.BlockSpec((TOP_K * tm,), lambda i: (i,), memory_space=pltpu.SMEM),
            pl.BlockSpec(unused8.shape, lambda i: (0,), memory_space=pltpu.SMEM),
            pl.BlockSpec((tm * ROW_TILE, LANES), row),
            pl.BlockSpec((tm, d), row),
            pl.BlockSpec((tm, ple_dim), row),
            pl.BlockSpec((d, d), full2),
            pl.BlockSpec((ple_dim, d), full2),
        ],
        out_specs=[pl.BlockSpec(memory_space=pl.ANY), pl.BlockSpec((tm, d), row)],
        out_shape=[jax.ShapeDtypeStruct((n_slots * ROW_TILE, LANES), F32), jax.ShapeDtypeStruct((t, d), F32)],
        scratch_shapes=[pltpu.SemaphoreType.DMA(()), pltpu.SemaphoreType.DMA(())],
        compiler_params=_params(),
        name="dispatch",
    )(dest8, unused8, x1t, x1, p2, w_ple_gate, w_ple_proj)


def _expert_kernel(be_ref, bsrc_ref, bv_ref, xs_ref, wup_ref, bup_ref, wdn_ref, bdn_ref, ys_ref,
                   wup_bf, wdn_bf, *, bm, d_ff):
    b = pl.program_id(0)
    nv = bv_ref[b]

    @pl.when(jnp.logical_or(b == 0, be_ref[b] != be_ref[jnp.maximum(b - 1, 0)]))
    def _():
        wup_bf[...] = wup_ref[0].astype(BF16)
        wdn_bf[...] = wdn_ref[0].astype(BF16)

    @pl.when(nv > 0)
    def _():
        rows = lax.broadcasted_iota(I32, (bm, 1), 0)
        x = jnp.where(rows < nv, _load_token_tiles(xs_ref, bm), 0.0).astype(BF16)
        hid = jnp.dot(x, wup_bf[...], preferred_element_type=F32) + bup_ref[0]
        glu = jnp.minimum(hid[:, :d_ff], SWIGLU_LIMIT)
        lin = jnp.clip(hid[:, d_ff:], -SWIGLU_LIMIT, SWIGLU_LIMIT)
        act = glu * _sigmoid(SWIGLU_ALPHA * glu) * (lin + 1.0)
        y = jnp.dot(act.astype(BF16), wdn_bf[...], preferred_element_type=F32) + bdn_ref[0]
        _store_token_tiles(ys_ref, y)

    @pl.when(nv == 0)
    def _():
        ys_ref[...] = jnp.zeros(ys_ref.shape, F32)


def _experts(be, bsrc, bv, xs, w_up, b_up, w_down, b_down):
    bm = EXPERT_BLOCK
    n_slots = xs.shape[0] // ROW_TILE
    d_ff, d = w_down.shape[1:]
    kern = functools.partial(_expert_kernel, bm=bm, d_ff=d_ff)
    blk = lambda b, be_r, bsrc_r, bv_r: (bsrc_r[b], 0)
    wsel = lambda b, be_r, bsrc_r, bv_r: (be_r[b], 0, 0)
    grid_spec = pltpu.PrefetchScalarGridSpec(
        num_scalar_prefetch=3,
        grid=(n_slots // bm,),
        in_specs=[
            pl.BlockSpec((bm * ROW_TILE, LANES), blk),
            pl.BlockSpec((1, d, 2 * d_ff), wsel),
            pl.BlockSpec((1, 1, 2 * d_ff), wsel),
            pl.BlockSpec((1, d_ff, d), wsel),
            pl.BlockSpec((1, 1, d), wsel),
        ],
        out_specs=pl.BlockSpec((bm * ROW_TILE, LANES), lambda b, be_r, bsrc_r, bv_r: (b, 0)),
        scratch_shapes=[pltpu.VMEM((d, 2 * d_ff), BF16), pltpu.VMEM((d_ff, d), BF16)],
    )
    return pl.pallas_call(
        kern,
        grid_spec=grid_spec,
        out_shape=jax.ShapeDtypeStruct((n_slots * ROW_TILE, LANES), F32),
        compiler_params=_params(),
        name="experts",
    )(be, bsrc, bv, xs, w_up, b_up, w_down, b_down)


def _final_kernel(dest_ref, dnext_ref, x1_ref, ple_ref, gate_ref, g2_ref, b2_ref, ys_hbm,
                  o_ref, ybuf, sem, *, tm, alpha):
    i = pl.program_id(0)
    slot = i % 2

    def gather(ids_ref, buf_slot):
        def body(t, carry):
            for kk in range(TOP_K):
                _tile_copy(ys_hbm, ids_ref[TOP_K * t + kk], ybuf.at[buf_slot, kk], t * ROW_TILE,
                           sem.at[buf_slot]).start(priority=kk % 2)
            return carry

        lax.fori_loop(0, tm, body, 0, unroll=ISSUE_UNROLL)

    @pl.when(i == 0)
    def _():
        gather(dest_ref, 0)

    @pl.when(i + 1 < pl.num_programs(0))
    def _():
        gather(dnext_ref, 1 - slot)

    for kk in range(TOP_K):
        pltpu.make_async_copy(ys_hbm.at[pl.ds(0, tm * ROW_TILE), :], ybuf.at[slot, kk], sem.at[slot]).wait()
    gate = gate_ref[...]
    ffn = gate[:, 0:1] * _load_token_tiles(ybuf.at[slot, 0], tm)
    for kk in range(1, TOP_K):
        ffn = ffn + gate[:, kk:kk + 1] * _load_token_tiles(ybuf.at[slot, kk], tm)
    o_ref[...] = _layer_norm(alpha * x1_ref[...] + ffn + ple_ref[...], g2_ref[...], b2_ref[...])


def _final(dest, x1, ple, gate, ln_g, ln_b, ys, alpha):
    t, d = x1.shape
    tm = COMBINE_TILE
    n_tiles = t // tm
    row = lambda i: (i, 0)
    full2 = lambda i: (0, 0)
    kern = functools.partial(_final_kernel, tm=tm, alpha=alpha)
    return pl.pallas_call(
        kern,
        grid=(n_tiles,),
        in_specs=[
            pl.BlockSpec((TOP_K * tm,), lambda i: (i,), memory_space=pltpu.SMEM),
            pl.BlockSpec((TOP_K * tm,), lambda i: (jnp.minimum(i + 1, n_tiles - 1),), memory_space=pltpu.SMEM),
            pl.BlockSpec((tm, d), row),
            pl.BlockSpec((tm, d), row),
            pl.BlockSpec((tm, TOP_K), row),
            pl.BlockSpec((1, d), full2),
            pl.BlockSpec((1, d), full2),
            pl.BlockSpec(memory_space=pl.ANY),
        ],
        out_specs=pl.BlockSpec((tm, d), row),
        out_shape=jax.ShapeDtypeStruct((t, d), F32),
        scratch_shapes=[pltpu.VMEM((2, TOP_K, tm * ROW_TILE, LANES), F32), pltpu.SemaphoreType.DMA((2,))],
        compiler_params=_params(),
        name="combine_ln2",
    )(dest, dest, x1, ple, gate, ln_g, ln_b, ys)


def _augment_k_weight(wk):
    d = wk.shape[0]
    w = wk.reshape(d, N_DIFF_HEADS * 2, DIFF_HEAD_DIM)
    w = jnp.pad(w, ((0, 0), (0, 0), (0, HEAD_WIDTH - DIFF_HEAD_DIM)))
    return w.reshape(d, N_DIFF_HEADS * 2 * HEAD_WIDTH)


def _alibi_key_bias(seq):
    slopes = np.asarray([2.0 ** (-8.0 * (h + 1) / N_DIFF_HEADS) for h in range(N_DIFF_HEADS)], np.float32)
    b = slopes[None, :] * np.arange(seq, dtype=np.float32)[:, None] * np.float32(LOG2E)
    pieces = []
    rem = b
    for _ in range(N_BIAS_PIECES):
        piece = (rem.view(np.uint32) & np.uint32(0xFFFF0000)).view(np.float32)
        pieces.append(piece)
        rem = rem - piece
    pb = jnp.asarray(np.stack(pieces, axis=-1)).astype(BF16)
    pb = jnp.broadcast_to(pb[:, :, None, :], (seq, N_DIFF_HEADS, 2, N_BIAS_PIECES))
    pb = jnp.pad(pb, ((0, 0), (0, 0), (0, 0), (DIFF_HEAD_DIM, HEAD_WIDTH - DIFF_HEAD_DIM - N_BIAS_PIECES)))
    return pb.reshape(seq, N_DIFF_HEADS * 2 * HEAD_WIDTH)


def _split_bf16(w):
    hi = lax.bitcast_convert_type(lax.bitcast_convert_type(w, jnp.uint32) & jnp.uint32(0xFFFF0000), F32)
    return hi.astype(BF16), (w - hi).astype(BF16)


def _routing_tables(counts, topi, rank, n_blocks):
    bm = EXPERT_BLOCK
    n_blk_e = (counts + bm - 1) // bm
    cum_blk = jnp.cumsum(n_blk_e)
    blk_start = cum_blk - n_blk_e
    n_used = cum_blk[-1]
    experts = jnp.arange(N_EXPERTS, dtype=I32)
    slot_start = jnp.sum(jnp.where(topi[..., None] == experts, blk_start * bm, 0), axis=-1)
    dest = (slot_start + rank).reshape(-1).astype(I32)
    bidx = jnp.arange(n_blocks, dtype=I32)
    bsrc = jnp.minimum(bidx, n_used - 1).astype(I32)
    be = jnp.minimum(jnp.sum((cum_blk[None, :] <= bsrc[:, None]).astype(I32), axis=-1), N_EXPERTS - 1)
    valid = jnp.clip(counts[be] - (bsrc - blk_start[be]) * bm, 0, bm)
    bv = jnp.where(bidx < n_used, valid, 0).astype(I32)
    seg_start = jnp.concatenate([blk_start * bm + counts, (n_used * bm)[None]])
    seg_size = jnp.concatenate([n_blk_e * bm - counts, ((n_blocks - n_used) * bm)[None]])
    seg_cum = jnp.cumsum(seg_size)
    j = jnp.arange(n_blocks * bm - dest.shape[0], dtype=I32)
    g = jnp.sum((seg_cum[None, :] <= j[:, None]).astype(I32), axis=-1)
    onehot = g[:, None] == jnp.arange(N_EXPERTS + 1, dtype=I32)[None, :]
    unused = (j + jnp.sum(jnp.where(onehot, seg_start - (seg_cum - seg_size), 0), axis=-1)).astype(I32)
    return dest, unused, be, bsrc, bv


def kernel(x, p, w_in, b_gate, lambda_q1, lambda_k1, lambda_q2, lambda_k2, subln_g, w_attn_br,
           w_pool_mix, pool_scale, w_pool_br, w_out, ln1_g, ln1_b, w_router, b_router, w_up, b_up,
           w_down, b_down, w_ple_gate, w_ple_proj, ln2_g, ln2_b):
    bsz, seq, d = x.shape
    depth = w_in.shape[0]
    t = bsz * seq
    assert seq % ATTN_TILE == 0 and seq % TOKEN_TILE == 0 and t % EXPERT_BLOCK == 0
    alpha = (2 * depth) ** 0.25
    n_blocks = (t * TOP_K) // EXPERT_BLOCK + N_EXPERTS
    n_slots = n_blocks * EXPERT_BLOCK

    x2 = x.reshape(t, d)
    for i in range(depth):
        lam_init = 0.8 - 0.6 * math.exp(-0.3 * i)
        wq, wk, wv, wu, wg = jnp.split(w_in[i].astype(BF16), IN_SPLITS, axis=-1)
        qt, kaug, vt, ga, gp = _inproj(x2, wq.T, _augment_k_weight(wk), wv.T, wu, wg, _alibi_key_bias(seq),
                                       b_gate[i][None], w_pool_mix[i].astype(BF16), pool_scale[i][None],
                                       w_pool_br[i].astype(BF16), bsz, seq)
        lvec = jnp.stack([lambda_q1[i], lambda_k1[i], lambda_q2[i], lambda_k2[i]]).astype(F32)
        o = _attention(qt, kaug, vt, lvec, subln_g[i][:, None], lam_init)
        x1, x1t, topi, rank, gate, cnt = _mix(o.reshape(t, ATTN_WIDTH), ga, gp, x2,
                                         w_attn_br[i].astype(BF16), w_out[i].astype(BF16),
                                         ln1_g[i][None], ln1_b[i][None], *_split_bf16(w_router[i]),
                                         b_router[i][None], alpha)
        counts = cnt[0].astype(I32)
        dest, unused, be, bsrc, bv = _routing_tables(counts, topi, rank, n_blocks)
        dest8 = dest * ROW_TILE
        xs, ple = _dispatch(dest8, unused * ROW_TILE, x1t, x1, p[i].reshape(t, -1),
                            w_ple_gate[i].astype(BF16), w_ple_proj[i].astype(BF16), n_slots)
        ys = _experts(be, bsrc, bv, xs, w_up[i], b_up[i][:, None, :], w_down[i], b_down[i][:, None, :])
        x2 = _final(dest8, x1, ple, gate, ln2_g[i][None], ln2_b[i][None], ys, alpha)
    return x2.reshape(bsz, seq, d)
```

```python
import functools
import math

import jax
import jax.numpy as jnp
from jax import lax
from jax.experimental import pallas as pl
from jax.experimental.pallas import tpu as pltpu

F32 = jnp.float32
BF16 = jnp.bfloat16
I32 = jnp.int32

N_DIFF_HEADS = 4
DIFF_HEAD_DIM = 64
HEAD_WIDTH = 2 * DIFF_HEAD_DIM
ATTN_WIDTH = N_DIFF_HEADS * HEAD_WIDTH
POOL_WINDOWS = (2, 4, 8, 16)
POOL_GROUP_DIM = 128
POOL_WIDTH = len(POOL_WINDOWS) * POOL_GROUP_DIM
POOL_HALO = 16
N_EXPERTS = 32
TOP_K = 4
SWIGLU_LIMIT = 7.0
SWIGLU_ALPHA = 1.702
LN_EPS = 1e-5
LANES = 128
ROW_TILE = 8
LOG2E = math.log2(math.e)
N_BIAS_PIECES = 3
IN_SPLITS = (ATTN_WIDTH, 2 * ATTN_WIDTH, 3 * ATTN_WIDTH, 3 * ATTN_WIDTH + POOL_WIDTH)

TOKEN_TILE = 512
ATTN_TILE = 512
SUM_ROWS = 16
EXPERT_BLOCK = 512
COMBINE_TILE = 512
ISSUE_UNROLL = 8
VMEM_LIMIT = 56 * 1024 * 1024


def _sigmoid(z):
    return 1.0 / (1.0 + jnp.exp(-z))


def _layer_norm(z, g, b):
    mu = jnp.mean(z, axis=-1, keepdims=True)
    zc = z - mu
    var = jnp.mean(zc * zc, axis=-1, keepdims=True)
    return zc * lax.rsqrt(var + LN_EPS) * g + b


def _params(n_axes=1):
    return pltpu.CompilerParams(dimension_semantics=("arbitrary",) * n_axes,
                                vmem_limit_bytes=VMEM_LIMIT)


def _inproj_kernel(x_ref, wq_ref, wk_ref, wv_ref, wu_ref, wg_ref, bg_ref, wmix_ref, pscale_ref,
                   wpbr_ref, qt_ref, ka_ref, vt_ref, ga_ref, gp_ref, ubuf, *, tiles_per_seq, tm, d_model):
    j = pl.program_id(0) % tiles_per_seq
    xb = x_ref[...].astype(BF16)

    def proj(w_ref, lo=None, hi=None):
        w = w_ref[...] if lo is None else w_ref[:, lo:hi]
        return jnp.dot(xb, w, preferred_element_type=F32)

    def proj_t(wt_ref):
        return lax.dot_general(wt_ref[...], xb, (((1,), (1,)), ((), ())), preferred_element_type=F32)

    qt_ref[...] = (proj_t(wq_ref) * (DIFF_HEAD_DIM ** -0.5 * LOG2E)).astype(BF16)
    k = proj(wk_ref)
    lane = lax.broadcasted_iota(I32, (tm, HEAD_WIDTH), 1)
    k_pos = (j * tm + lax.broadcasted_iota(I32, (tm, HEAD_WIDTH), 0)).astype(F32)
    for h in range(N_DIFF_HEADS):
        rem = k_pos * (2.0 ** (-8.0 * (h + 1) / N_DIFF_HEADS) * LOG2E)
        bias = jnp.zeros((tm, HEAD_WIDTH), F32)
        for piece in range(N_BIAS_PIECES):
            part = rem.astype(BF16).astype(F32)
            bias = jnp.where(lane == DIFF_HEAD_DIM + piece, part, bias)
            rem = rem - part
        kh = k[:, h * HEAD_WIDTH:(h + 1) * HEAD_WIDTH]
        for m, feats in enumerate((kh, pltpu.roll(kh, DIFF_HEAD_DIM, axis=1))):
            lo = (2 * h + m) * HEAD_WIDTH
            ka_ref[:, lo:lo + HEAD_WIDTH] = jnp.where(lane < DIFF_HEAD_DIM, feats, bias).astype(BF16)
    vt_ref[...] = proj_t(wv_ref).astype(BF16)
    u = proj(wu_ref)

    @pl.when(j == 0)
    def _():
        ubuf[0:POOL_HALO, :] = jnp.zeros((POOL_HALO, POOL_WIDTH), F32)

    ubuf[POOL_HALO:, :] = u
    pos = j * tm + lax.broadcasted_iota(I32, (tm, 1), 0)
    parts = []
    for gi, w in enumerate(POOL_WINDOWS):
        lo, hi = gi * POOL_GROUP_DIM, (gi + 1) * POOL_GROUP_DIM
        a = ubuf[:, lo:hi]
        s = a
        sh = 1
        while sh < w:
            s = s + pltpu.roll(s, sh, axis=0)
            sh *= 2
        cnt = jnp.minimum(pos + 1, w).astype(F32)
        pooled = s[POOL_HALO:, :] / cnt - a[POOL_HALO:, :]
        mixed = jnp.dot(pooled.astype(BF16), wmix_ref[gi], preferred_element_type=F32)
        parts.append(mixed * pscale_ref[:, lo:hi])
    ubuf[0:POOL_HALO, :] = u[tm - POOL_HALO:, :]
    p_branch = jnp.dot(jnp.concatenate(parts, axis=-1).astype(BF16), wpbr_ref[...],
                       preferred_element_type=F32)

    g_a = _sigmoid(proj(wg_ref, 0, d_model) + bg_ref[:, 0:d_model])
    ga_ref[...] = g_a.astype(BF16)
    g_p = _sigmoid(proj(wg_ref, d_model, 2 * d_model) + bg_ref[:, d_model:2 * d_model])
    gp_ref[...] = (g_p * p_branch).astype(BF16)


def _inproj(x2, wq_t, wk, wv_t, wu, wg, b_gate, w_mix, pool_scale, w_pool_br, bsz, seq):
    t, d = x2.shape
    tm = ATTN_TILE
    n_t = seq // tm
    row = lambda i: (i, 0)
    full2 = lambda i: (0, 0)
    blk4 = lambda i: (i // n_t, i % n_t, 0, 0)
    kern = functools.partial(_inproj_kernel, tiles_per_seq=n_t, tm=tm, d_model=d)
    ka_w = N_DIFF_HEADS * 2 * HEAD_WIDTH
    return pl.pallas_call(
        kern,
        grid=(t // tm,),
        in_specs=[
            pl.BlockSpec((tm, d), row),
            pl.BlockSpec(wq_t.shape, full2),
            pl.BlockSpec(wk.shape, full2),
            pl.BlockSpec(wv_t.shape, full2),
            pl.BlockSpec(wu.shape, full2),
            pl.BlockSpec(wg.shape, full2),
            pl.BlockSpec((1, 2 * d), full2),
            pl.BlockSpec(w_mix.shape, lambda i: (0, 0, 0)),
            pl.BlockSpec((1, POOL_WIDTH), full2),
            pl.BlockSpec((POOL_WIDTH, d), full2),
        ],
        out_specs=[
            pl.BlockSpec((None, None, ATTN_WIDTH, tm), blk4),
            pl.BlockSpec((None, None, tm, ka_w), blk4),
            pl.BlockSpec((None, None, ATTN_WIDTH, tm), blk4),
            pl.BlockSpec((tm, d), row),
            pl.BlockSpec((tm, d), row),
        ],
        out_shape=[
            jax.ShapeDtypeStruct((bsz, n_t, ATTN_WIDTH, tm), BF16),
            jax.ShapeDtypeStruct((bsz, n_t, tm, ka_w), BF16),
            jax.ShapeDtypeStruct((bsz, n_t, ATTN_WIDTH, tm), BF16),
            jax.ShapeDtypeStruct((t, d), BF16),
            jax.ShapeDtypeStruct((t, d), BF16),
        ],
        scratch_shapes=[pltpu.VMEM((POOL_HALO + tm, POOL_WIDTH), F32)],
        compiler_params=_params(),
        name="inproj",
    )(x2, wq_t, wk, wv_t, wu, wg, b_gate, w_mix, pool_scale, w_pool_br)


def _attn_kernel(lvec_ref, sg_ref, qt_ref, ka_ref, vt_ref, o_ref, m_scr, acc_scr, za_scr, zb_scr,
                 *, tq, lam_init):
    qi = pl.program_id(2)
    hd = DIFF_HEAD_DIM
    qt = qt_ref[...]
    sub = lax.broadcasted_iota(I32, (hd, tq), 0)
    ones_rows = jnp.where(sub < N_BIAS_PIECES, 1.0, 0.0).astype(BF16)
    qa = (jnp.concatenate([qt[0:hd], ones_rows], axis=0), jnp.concatenate([qt[hd:2 * hd], ones_rows], axis=0))

    m_scr[...] = jnp.full(m_scr.shape, -jnp.inf, F32)
    acc_scr[...] = jnp.zeros(acc_scr.shape, F32)
    sum_rows = jnp.ones((SUM_ROWS, tq), BF16)

    def scores(ki, z_ref):
        for m in range(2):
            ka = ka_ref[ki, :, m * HEAD_WIDTH:(m + 1) * HEAD_WIDTH]
            z_ref[m] = jnp.dot(ka, qa[m], preferred_element_type=F32)

    def consume(ki, z_ref, masked):
        vt = jnp.concatenate([vt_ref[ki], sum_rows], axis=0)
        for m in range(2):
            z = z_ref[m]
            if masked:
                r = lax.broadcasted_iota(I32, z.shape, 0)
                c = lax.broadcasted_iota(I32, z.shape, 1)
                z = jnp.where(r <= c, z, -jnp.inf)
            m_old = m_scr[m]
            m_new = jnp.maximum(m_old, jnp.max(z, axis=0, keepdims=True))
            alpha = jnp.exp2(m_old - m_new)
            p = jnp.exp2(z - m_new).astype(BF16)
            acc_scr[m] = alpha * acc_scr[m] + jnp.dot(vt, p, preferred_element_type=F32)
            m_scr[m] = m_new

    scores(0, za_scr)

    def pair(j, carry):
        ki = 2 * j
        scores(ki + 1, zb_scr)
        consume(ki, za_scr, False)
        scores(ki + 2, za_scr)
        consume(ki + 1, zb_scr, False)
        return carry

    lax.fori_loop(0, qi // 2, pair, 0)

    @pl.when(qi % 2 == 0)
    def _():
        consume(qi, za_scr, True)

    @pl.when(qi % 2 == 1)
    def _():
        scores(qi, zb_scr)
        consume(qi - 1, za_scr, False)
        consume(qi, zb_scr, True)

    lv = lvec_ref[...]
    lam = (jnp.exp(jnp.sum(lv[0:1] * lv[1:2], axis=-1, keepdims=True))
           - jnp.exp(jnp.sum(lv[2:3] * lv[3:4], axis=-1, keepdims=True)) + lam_init)
    hw = HEAD_WIDTH
    o = (acc_scr[0, 0:hw] / acc_scr[0, hw:hw + 1]
         - lam * (acc_scr[1, 0:hw] / acc_scr[1, hw:hw + 1]))
    ms = jnp.mean(o * o, axis=0, keepdims=True)
    o = o * lax.rsqrt(ms + LN_EPS) * sg_ref[...] * (1.0 - lam_init)
    o_ref[...] = o.T.astype(BF16)


def _attention(qt, kaug, vt, lvec, subln_g_col, lam_init):
    b, n_t, _, tq = qt.shape
    kern = functools.partial(_attn_kernel, tq=tq, lam_init=lam_init)
    return pl.pallas_call(
        kern,
        grid=(b, N_DIFF_HEADS, n_t),
        in_specs=[
            pl.BlockSpec(lvec.shape, lambda bi, h, i: (0, 0)),
            pl.BlockSpec((HEAD_WIDTH, 1), lambda bi, h, i: (0, 0)),
            pl.BlockSpec((None, None, HEAD_WIDTH, tq), lambda bi, h, i: (bi, i, h, 0)),
            pl.BlockSpec((None, n_t, tq, 2 * HEAD_WIDTH), lambda bi, h, i: (bi, 0, 0, h)),
            pl.BlockSpec((None, n_t, HEAD_WIDTH, tq), lambda bi, h, i: (bi, 0, h, 0)),
        ],
        out_specs=pl.BlockSpec((None, tq, HEAD_WIDTH), lambda bi, h, i: (bi, i, h)),
        out_shape=jax.ShapeDtypeStruct((b, n_t * tq, ATTN_WIDTH), BF16),
        scratch_shapes=[
            pltpu.VMEM((2, 1, tq), F32),
            pltpu.VMEM((2, HEAD_WIDTH + SUM_ROWS, tq), F32),
            pltpu.VMEM((2, tq, tq), F32),
            pltpu.VMEM((2, tq, tq), F32),
        ],
        compiler_params=_params(3),
        name="diffattn",
    )(lvec, subln_g_col, qt, kaug, vt)


def _mix_kernel(o_ref, ga_ref, gp_ref, x_ref, wabr_ref, wout_ref, g1_ref, b1_ref, wrh_ref, wrl_ref, br_ref,
                x1_ref, x1t_ref, topi_ref, rank_ref, gate_ref, cnt_ref, carry, *, tm, alpha):
    i = pl.program_id(0)

    @pl.when(i == 0)
    def _():
        carry[...] = jnp.zeros(carry.shape, F32)

    a_branch = jnp.dot(o_ref[...], wabr_ref[...], preferred_element_type=F32)
    mixed = ga_ref[...].astype(F32) * a_branch + gp_ref[...].astype(F32)
    y = jnp.dot(mixed.astype(BF16), wout_ref[...], preferred_element_type=F32)
    x1 = _layer_norm(alpha * x_ref[...] + y, g1_ref[...], b1_ref[...])
    x1_ref[...] = x1
    _store_token_tiles(x1t_ref, x1)

    x_hi = x1.astype(BF16)
    x_lo = (x1 - x_hi.astype(F32)).astype(BF16)
    logits = (jnp.dot(x_hi, wrh_ref[...], preferred_element_type=F32)
              + jnp.dot(x_lo, wrh_ref[...], preferred_element_type=F32)
              + jnp.dot(x_hi, wrl_ref[...], preferred_element_type=F32)) + br_ref[...]
    e_iota = lax.broadcasted_iota(I32, logits.shape, 1).astype(F32)
    vals, idxs = [], []
    rem = logits
    for _ in range(TOP_K):
        m = jnp.max(rem, axis=-1, keepdims=True)
        idx = jnp.min(jnp.where(rem == m, e_iota, float(N_EXPERTS)), axis=-1, keepdims=True)
        vals.append(m)
        idxs.append(idx)
        rem = jnp.where(e_iota == idx, -jnp.inf, rem)
    exps = [jnp.exp(vk - vals[0]) for vk in vals]
    den = exps[0] + exps[1] + exps[2] + exps[3]

    sel = jnp.zeros(logits.shape, F32)
    for idx in idxs:
        sel = sel + jnp.where(e_iota == idx, 1.0, 0.0)
    r = lax.broadcasted_iota(I32, (tm, tm), 0)
    c = lax.broadcasted_iota(I32, (tm, tm), 1)
    tri = jnp.where(c < r, 1.0, 0.0).astype(BF16)
    prefix = jnp.dot(tri, sel.astype(BF16), preferred_element_type=F32) + carry[...]
    carry[...] = carry[...] + jnp.sum(sel, axis=0, keepdims=True)
    cnt_ref[...] = carry[...]

    lane = lax.broadcasted_iota(I32, (tm, TOP_K), 1)
    topi = jnp.zeros((tm, TOP_K), F32)
    rank = jnp.zeros((tm, TOP_K), F32)
    gate = jnp.zeros((tm, TOP_K), F32)
    for kk in range(TOP_K):
        rk = jnp.sum(jnp.where(e_iota == idxs[kk], prefix, 0.0), axis=-1, keepdims=True)
        topi = jnp.where(lane == kk, idxs[kk], topi)
        rank = jnp.where(lane == kk, rk, rank)
        gate = jnp.where(lane == kk, exps[kk] / den, gate)
    topi_ref[...] = topi.astype(I32)
    rank_ref[...] = rank.astype(I32)
    gate_ref[...] = gate


def _mix(o2, ga, gp, x2, w_attn_br, w_out, ln_g, ln_b, w_router_hi, w_router_lo, b_router, alpha):
    t, d = x2.shape
    tm = TOKEN_TILE
    row = lambda i: (i, 0)
    full2 = lambda i: (0, 0)
    kern = functools.partial(_mix_kernel, tm=tm, alpha=alpha)
    return pl.pallas_call(
        kern,
        grid=(t // tm,),
        in_specs=[
            pl.BlockSpec((tm, ATTN_WIDTH), row),
            pl.BlockSpec((tm, d), row),
            pl.BlockSpec((tm, d), row),
            pl.BlockSpec((tm, d), row),
            pl.BlockSpec((ATTN_WIDTH, d), full2),
            pl.BlockSpec((d, d), full2),
            pl.BlockSpec((1, d), full2),
            pl.BlockSpec((1, d), full2),
            pl.BlockSpec((d, N_EXPERTS), full2),
            pl.BlockSpec((d, N_EXPERTS), full2),
            pl.BlockSpec((1, N_EXPERTS), full2),
        ],
        out_specs=[
            pl.BlockSpec((tm, d), row),
            pl.BlockSpec((tm * ROW_TILE, LANES), row),
            pl.BlockSpec((tm, TOP_K), row),
            pl.BlockSpec((tm, TOP_K), row),
            pl.BlockSpec((tm, TOP_K), row),
            pl.BlockSpec((1, N_EXPERTS), full2),
        ],
        out_shape=[
            jax.ShapeDtypeStruct((t, d), F32),
            jax.ShapeDtypeStruct((t * ROW_TILE, LANES), F32),
            jax.ShapeDtypeStruct((t, TOP_K), I32),
            jax.ShapeDtypeStruct((t, TOP_K), I32),
            jax.ShapeDtypeStruct((t, TOP_K), F32),
            jax.ShapeDtypeStruct((1, N_EXPERTS), F32),
        ],
        scratch_shapes=[pltpu.VMEM((1, N_EXPERTS), F32)],
        compiler_params=_params(),
        name="mix_ln1_router",
    )(o2, ga, gp, x2, w_attn_br, w_out, ln_g, ln_b, w_router_hi, w_router_lo, b_router)


def _store_token_tiles(ref, val):
    rows = val.shape[0]
    for s in range(ROW_TILE):
        ref[pl.ds(s, rows, stride=ROW_TILE), :] = val[:, s * LANES:(s + 1) * LANES]


def _load_token_tiles(ref, rows):
    return jnp.concatenate([ref[pl.ds(s, rows, stride=ROW_TILE), :] for s in range(ROW_TILE)], axis=-1)


def _tile_copy(src_ref, src_row8, dst_ref, dst_row8, sem):
    return pltpu.make_async_copy(src_ref.at[pl.ds(pl.multiple_of(src_row8, ROW_TILE), ROW_TILE), :],
                                 dst_ref.at[pl.ds(pl.multiple_of(dst_row8, ROW_TILE), ROW_TILE), :], sem)


def _dispatch_kernel(dest_ref, unused_ref, x1t_ref, x1_ref, p_ref, wg_ref, wp_ref, xs_hbm, ple_ref, sem, fill_sem,
                     *, tm):
    n_unused = unused_ref.shape[0]

    @pl.when(pl.program_id(0) == 0)
    def _():
        def fill(r, carry):
            _tile_copy(x1t_ref, 0, xs_hbm, unused_ref[r], fill_sem).start(priority=1)
            return carry

        lax.fori_loop(0, n_unused, fill, 0, unroll=ISSUE_UNROLL)
        for _ in range(n_unused // tm):
            pltpu.make_async_copy(x1t_ref, xs_hbm.at[pl.ds(0, tm * ROW_TILE), :], fill_sem).wait()

    def body(t, carry):
        for kk in range(TOP_K):
            _tile_copy(x1t_ref, t * ROW_TILE, xs_hbm, dest_ref[TOP_K * t + kk], sem).start(priority=kk % 2)
        return carry

    lax.fori_loop(0, tm, body, 0, unroll=ISSUE_UNROLL)
    ple_ref[...] = (_sigmoid(jnp.dot(x1_ref[...].astype(BF16), wg_ref[...], preferred_element_type=F32))
                    * jnp.dot(p_ref[...].astype(BF16), wp_ref[...], preferred_element_type=F32))
    for _ in range(TOP_K):
        pltpu.make_async_copy(x1t_ref, xs_hbm.at[pl.ds(0, tm * ROW_TILE), :], sem).wait()


def _dispatch(dest8, unused8, x1t, x1, p2, w_ple_gate, w_ple_proj, n_slots):
    tm = TOKEN_TILE
    t, d = x1.shape
    ple_dim = p2.shape[1]
    row = lambda i: (i, 0)
    full2 = lambda i: (0, 0)
    kern = functools.partial(_dispatch_kernel, tm=tm)
    return pl.pallas_call(
        kern,
        grid=(t // tm,),
        in_specs=[
            pl.BlockSpec((TOP_K * tm,), lambda i: (i,), memory_space=pltpu.SMEM),
            pl.BlockSpec(unused8.shape, lambda i: (0,), memory_space=pltpu.SMEM),
            pl.BlockSpec((tm * ROW_TILE, LANES), row),
            pl.BlockSpec((tm, d), row),
            pl.BlockSpec((tm, ple_dim), row),
            pl.BlockSpec((d, d), full2),
            pl.BlockSpec((ple_dim, d), full2),
        ],
        out_specs=[pl.BlockSpec(memory_space=pl.ANY), pl.BlockSpec((tm, d), row)],
        out_shape=[jax.ShapeDtypeStruct((n_slots * ROW_TILE, LANES), F32), jax.ShapeDtypeStruct((t, d), F32)],
        scratch_shapes=[pltpu.SemaphoreType.DMA(()), pltpu.SemaphoreType.DMA(())],
        compiler_params=_params(),
        name="dispatch",
    )(dest8, unused8, x1t, x1, p2, w_ple_gate, w_ple_proj)


def _expert_kernel(be_ref, bsrc_ref, bv_ref, xs_ref, wup_ref, bup_ref, wdn_ref, bdn_ref, ys_ref,
                   wup_bf, wdn_bf, *, bm, d_ff):
    b = pl.program_id(0)
    nv = bv_ref[b]

    @pl.when(jnp.logical_or(b == 0, be_ref[b] != be_ref[jnp.maximum(b - 1, 0)]))
    def _():
        wup_bf[...] = wup_ref[0].astype(BF16)
        wdn_bf[...] = wdn_ref[0].astype(BF16)

    @pl.when(nv > 0)
    def _():
        rows = lax.broadcasted_iota(I32, (bm, 1), 0)
        x = jnp.where(rows < nv, _load_token_tiles(xs_ref, bm), 0.0).astype(BF16)
        hid = jnp.dot(x, wup_bf[...], preferred_element_type=F32) + bup_ref[0]
        glu = jnp.minimum(hid[:, :d_ff], SWIGLU_LIMIT)
        lin = jnp.clip(hid[:, d_ff:], -SWIGLU_LIMIT, SWIGLU_LIMIT)
        act = glu * _sigmoid(SWIGLU_ALPHA * glu) * (lin + 1.0)
        y = jnp.dot(act.astype(BF16), wdn_bf[...], preferred_element_type=F32) + bdn_ref[0]
        _store_token_tiles(ys_ref, y)

    @pl.when(nv == 0)
    def _():
        ys_ref[...] = jnp.zeros(ys_ref.shape, F32)


def _experts(be, bsrc, bv, xs, w_up, b_up, w_down, b_down):
    bm = EXPERT_BLOCK
    n_slots = xs.shape[0] // ROW_TILE
    d_ff, d = w_down.shape[1:]
    kern = functools.partial(_expert_kernel, bm=bm, d_ff=d_ff)
    blk = lambda b, be_r, bsrc_r, bv_r: (bsrc_r[b], 0)
    wsel = lambda b, be_r, bsrc_r, bv_r: (be_r[b], 0, 0)
    grid_spec = pltpu.PrefetchScalarGridSpec(
        num_scalar_prefetch=3,
        grid=(n_slots // bm,),
        in_specs=[
            pl.BlockSpec((bm * ROW_TILE, LANES), blk),
            pl.BlockSpec((1, d, 2 * d_ff), wsel),
            pl.BlockSpec((1, 1, 2 * d_ff), wsel),
            pl.BlockSpec((1, d_ff, d), wsel),
            pl.BlockSpec((1, 1, d), wsel),
        ],
        out_specs=pl.BlockSpec((bm * ROW_TILE, LANES), lambda b, be_r, bsrc_r, bv_r: (b, 0)),
        scratch_shapes=[pltpu.VMEM((d, 2 * d_ff), BF16), pltpu.VMEM((d_ff, d), BF16)],
    )
    return pl.pallas_call(
        kern,
        grid_spec=grid_spec,
        out_shape=jax.ShapeDtypeStruct((n_slots * ROW_TILE, LANES), F32),
        compiler_params=_params(),
        name="experts",
    )(be, bsrc, bv, xs, w_up, b_up, w_down, b_down)


def _final_kernel(dest_ref, dnext_ref, x1_ref, ple_ref, gate_ref, g2_ref, b2_ref, ys_hbm,
                  o_ref, ybuf, sem, *, tm, alpha):
    i = pl.program_id(0)
    slot = i % 2

    def gather(ids_ref, buf_slot):
        def body(t, carry):
            for kk in range(TOP_K):
                _tile_copy(ys_hbm, ids_ref[TOP_K * t + kk], ybuf.at[buf_slot, kk], t * ROW_TILE,
                           sem.at[buf_slot]).start(priority=kk % 2)
            return carry

        lax.fori_loop(0, tm, body, 0, unroll=ISSUE_UNROLL)

    @pl.when(i == 0)
    def _():
        gather(dest_ref, 0)

    @pl.when(i + 1 < pl.num_programs(0))
    def _():
        gather(dnext_ref, 1 - slot)

    for kk in range(TOP_K):
        pltpu.make_async_copy(ys_hbm.at[pl.ds(0, tm * ROW_TILE), :], ybuf.at[slot, kk], sem.at[slot]).wait()
    gate = gate_ref[...]
    ffn = gate[:, 0:1] * _load_token_tiles(ybuf.at[slot, 0], tm)
    for kk in range(1, TOP_K):
        ffn = ffn + gate[:, kk:kk + 1] * _load_token_tiles(ybuf.at[slot, kk], tm)
    o_ref[...] = _layer_norm(alpha * x1_ref[...] + ffn + ple_ref[...], g2_ref[...], b2_ref[...])


def _final(dest, x1, ple, gate, ln_g, ln_b, ys, alpha):
    t, d = x1.shape
    tm = COMBINE_TILE
    n_tiles = t // tm
    row = lambda i: (i, 0)
    full2 = lambda i: (0, 0)
    kern = functools.partial(_final_kernel, tm=tm, alpha=alpha)
    return pl.pallas_call(
        kern,
        grid=(n_tiles,),
        in_specs=[
            pl.BlockSpec((TOP_K * tm,), lambda i: (i,), memory_space=pltpu.SMEM),
            pl.BlockSpec((TOP_K * tm,), lambda i: (jnp.minimum(i + 1, n_tiles - 1),), memory_space=pltpu.SMEM),
            pl.BlockSpec((tm, d), row),
            pl.BlockSpec((tm, d), row),
            pl.BlockSpec((tm, TOP_K), row),
            pl.BlockSpec((1, d), full2),
            pl.BlockSpec((1, d), full2),
            pl.BlockSpec(memory_space=pl.ANY),
        ],
        out_specs=pl.BlockSpec((tm, d), row),
        out_shape=jax.ShapeDtypeStruct((t, d), F32),
        scratch_shapes=[pltpu.VMEM((2, TOP_K, tm * ROW_TILE, LANES), F32), pltpu.SemaphoreType.DMA((2,))],
        compiler_params=_params(),
        name="combine_ln2",
    )(dest, dest, x1, ple, gate, ln_g, ln_b, ys)


def _split_bf16(w):
    hi = lax.bitcast_convert_type(lax.bitcast_convert_type(w, jnp.uint32) & jnp.uint32(0xFFFF0000), F32)
    return hi.astype(BF16), (w - hi).astype(BF16)


def _routing_tables(counts, topi, rank, n_blocks):
    bm = EXPERT_BLOCK
    n_blk_e = (counts + bm - 1) // bm
    cum_blk = jnp.cumsum(n_blk_e)
    blk_start = cum_blk - n_blk_e
    n_used = cum_blk[-1]
    experts = jnp.arange(N_EXPERTS, dtype=I32)
    slot_start = jnp.sum(jnp.where(topi[..., None] == experts, blk_start * bm, 0), axis=-1)
    dest = (slot_start + rank).reshape(-1).astype(I32)
    bidx = jnp.arange(n_blocks, dtype=I32)
    bsrc = jnp.minimum(bidx, n_used - 1).astype(I32)
    be = jnp.minimum(jnp.sum((cum_blk[None, :] <= bsrc[:, None]).astype(I32), axis=-1), N_EXPERTS - 1)
    valid = jnp.clip(counts[be] - (bsrc - blk_start[be]) * bm, 0, bm)
    bv = jnp.where(bidx < n_used, valid, 0).astype(I32)
    seg_start = jnp.concatenate([blk_start * bm + counts, (n_used * bm)[None]])
    seg_size = jnp.concatenate([n_blk_e * bm - counts, ((n_blocks - n_used) * bm)[None]])
    seg_cum = jnp.cumsum(seg_size)
    j = jnp.arange(n_blocks * bm - dest.shape[0], dtype=I32)
    g = jnp.sum((seg_cum[None, :] <= j[:, None]).astype(I32), axis=-1)
    onehot = g[:, None] == jnp.arange(N_EXPERTS + 1, dtype=I32)[None, :]
    unused = (j + jnp.sum(jnp.where(onehot, seg_start - (seg_cum - seg_size), 0), axis=-1)).astype(I32)
    return dest, unused, be, bsrc, bv


def kernel(x, p, w_in, b_gate, lambda_q1, lambda_k1, lambda_q2, lambda_k2, subln_g, w_attn_br,
           w_pool_mix, pool_scale, w_pool_br, w_out, ln1_g, ln1_b, w_router, b_router, w_up, b_up,
           w_down, b_down, w_ple_gate, w_ple_proj, ln2_g, ln2_b):
    bsz, seq, d = x.shape
    depth = w_in.shape[0]
    t = bsz * seq
    assert seq % ATTN_TILE == 0 and seq % TOKEN_TILE == 0 and t % EXPERT_BLOCK == 0
    alpha = (2 * depth) ** 0.25
    n_blocks = (t * TOP_K) // EXPERT_BLOCK + N_EXPERTS
    n_slots = n_blocks * EXPERT_BLOCK

    x2 = x.reshape(t, d)
    for i in range(depth):
        lam_init = 0.8 - 0.6 * math.exp(-0.3 * i)
        wq, wk, wv, wu, wg = jnp.split(w_in[i].astype(BF16), IN_SPLITS, axis=-1)
        qt, kaug, vt, ga, gp = _inproj(x2, wq.T, wk, wv.T, wu, wg,
                                       b_gate[i][None], w_pool_mix[i].astype(BF16), pool_scale[i][None],
                                       w_pool_br[i].astype(BF16), bsz, seq)
        lvec = jnp.stack([lambda_q1[i], lambda_k1[i], lambda_q2[i], lambda_k2[i]]).astype(F32)
        o = _attention(qt, kaug, vt, lvec, subln_g[i][:, None], lam_init)
        x1, x1t, topi, rank, gate, cnt = _mix(o.reshape(t, ATTN_WIDTH), ga, gp, x2,
                                         w_attn_br[i].astype(BF16), w_out[i].astype(BF16),
                                         ln1_g[i][None], ln1_b[i][None], *_split_bf16(w_router[i]),
                                         b_router[i][None], alpha)
        counts = cnt[0].astype(I32)
        dest, unused, be, bsrc, bv = _routing_tables(counts, topi, rank, n_blocks)
        dest8 = dest * ROW_TILE
        xs, ple = _dispatch(dest8, unused * ROW_TILE, x1t, x1, p[i].reshape(t, -1),
                            w_ple_gate[i].astype(BF16), w_ple_proj[i].astype(BF16), n_slots)
        ys = _experts(be, bsrc, bv, xs, w_up[i], b_up[i][:, None, :], w_down[i], b_down[i][:, None, :])
        x2 = _final(dest8, x1, ple, gate, ln2_g[i][None], ln2_b[i][None], ys, alpha)
    return x2.reshape(bsz, seq, d)
```

```python
import functools
import math

import jax
import jax.numpy as jnp
from jax import lax
from jax.experimental import pallas as pl
from jax.experimental.pallas import tpu as pltpu

F32 = jnp.float32
BF16 = jnp.bfloat16
I32 = jnp.int32

N_DIFF_HEADS = 4
DIFF_HEAD_DIM = 64
HEAD_WIDTH = 2 * DIFF_HEAD_DIM
ATTN_WIDTH = N_DIFF_HEADS * HEAD_WIDTH
POOL_WINDOWS = (2, 4, 8, 16)
POOL_GROUP_DIM = 128
POOL_WIDTH = len(POOL_WINDOWS) * POOL_GROUP_DIM
POOL_HALO = 16
N_EXPERTS = 32
TOP_K = 4
SWIGLU_LIMIT = 7.0
SWIGLU_ALPHA = 1.702
LN_EPS = 1e-5
LANES = 128
ROW_TILE = 8
LOG2E = math.log2(math.e)
N_BIAS_PIECES = 3
IN_SPLITS = (ATTN_WIDTH, 2 * ATTN_WIDTH, 3 * ATTN_WIDTH, 3 * ATTN_WIDTH + POOL_WIDTH)

TOKEN_TILE = 512
ATTN_TILE = 512
SUM_ROWS = 16
EXPERT_BLOCK = 512
COMBINE_TILE = 512
ISSUE_UNROLL = 8
VMEM_LIMIT = 56 * 1024 * 1024


def _sigmoid(z):
    return 1.0 / (1.0 + jnp.exp(-z))


def _layer_norm(z, g, b):
    mu = jnp.mean(z, axis=-1, keepdims=True)
    zc = z - mu
    var = jnp.mean(zc * zc, axis=-1, keepdims=True)
    return zc * lax.rsqrt(var + LN_EPS) * g + b


def _params(n_axes=1):
    return pltpu.CompilerParams(dimension_semantics=("arbitrary",) * n_axes,
                                vmem_limit_bytes=VMEM_LIMIT)


def _inproj_kernel(x_ref, wq_ref, wk_ref, wv_ref, wu_ref, wg_ref, bg_ref, wmix_ref, pscale_ref,
                   wpbr_ref, qt_ref, ka_ref, vt_ref, ga_ref, gp_ref, ubuf, *, tiles_per_seq, tm, d_model):
    j = pl.program_id(0) % tiles_per_seq
    xb = x_ref[...].astype(BF16)

    def proj(w_ref, lo=None, hi=None):
        w = w_ref[...] if lo is None else w_ref[:, lo:hi]
        return jnp.dot(xb, w, preferred_element_type=F32)

    def proj_t(wt_ref):
        return lax.dot_general(wt_ref[...], xb, (((1,), (1,)), ((), ())), preferred_element_type=F32)

    qt_ref[...] = (proj_t(wq_ref) * (DIFF_HEAD_DIM ** -0.5 * LOG2E)).astype(BF16)
    k = proj(wk_ref)
    lane = lax.broadcasted_iota(I32, (tm, HEAD_WIDTH), 1)
    k_pos = (j * tm + lax.broadcasted_iota(I32, (tm, HEAD_WIDTH), 0)).astype(F32)
    for h in range(N_DIFF_HEADS):
        rem = k_pos * (2.0 ** (-8.0 * (h + 1) / N_DIFF_HEADS) * LOG2E)
        bias = jnp.zeros((tm, HEAD_WIDTH), F32)
        for piece in range(N_BIAS_PIECES):
            part = rem.astype(BF16).astype(F32)
            bias = jnp.where(lane == DIFF_HEAD_DIM + piece, part, bias)
            rem = rem - part
        kh = k[:, h * HEAD_WIDTH:(h + 1) * HEAD_WIDTH]
        for m, feats in enumerate((kh, pltpu.roll(kh, DIFF_HEAD_DIM, axis=1))):
            lo = (2 * h + m) * HEAD_WIDTH
            ka_ref[:, lo:lo + HEAD_WIDTH] = jnp.where(lane < DIFF_HEAD_DIM, feats, bias).astype(BF16)
    vt_ref[...] = proj_t(wv_ref).astype(BF16)
    u = proj(wu_ref)

    @pl.when(j == 0)
    def _():
        ubuf[0:POOL_HALO, :] = jnp.zeros((POOL_HALO, POOL_WIDTH), F32)

    ubuf[POOL_HALO:, :] = u
    pos = j * tm + lax.broadcasted_iota(I32, (tm, 1), 0)
    parts = []
    for gi, w in enumerate(POOL_WINDOWS):
        lo, hi = gi * POOL_GROUP_DIM, (gi + 1) * POOL_GROUP_DIM
        a = ubuf[:, lo:hi]
        s = a
        sh = 1
        while sh < w:
            s = s + pltpu.roll(s, sh, axis=0)
            sh *= 2
        cnt = jnp.minimum(pos + 1, w).astype(F32)
        pooled = s[POOL_HALO:, :] / cnt - a[POOL_HALO:, :]
        mixed = jnp.dot(pooled.astype(BF16), wmix_ref[gi], preferred_element_type=F32)
        parts.append(mixed * pscale_ref[:, lo:hi])
    ubuf[0:POOL_HALO, :] = u[tm - POOL_HALO:, :]
    p_branch = jnp.dot(jnp.concatenate(parts, axis=-1).astype(BF16), wpbr_ref[...],
                       preferred_element_type=F32)

    g_a = _sigmoid(proj(wg_ref, 0, d_model) + bg_ref[:, 0:d_model])
    ga_ref[...] = g_a.astype(BF16)
    g_p = _sigmoid(proj(wg_ref, d_model, 2 * d_model) + bg_ref[:, d_model:2 * d_model])
    gp_ref[...] = (g_p * p_branch).astype(BF16)


def _inproj(x2, wq_t, wk, wv_t, wu, wg, b_gate, w_mix, pool_scale, w_pool_br, bsz, seq):
    t, d = x2.shape
    tm = ATTN_TILE
    n_t = seq // tm
    row = lambda i: (i, 0)
    full2 = lambda i: (0, 0)
    blk4 = lambda i: (i // n_t, i % n_t, 0, 0)
    kern = functools.partial(_inproj_kernel, tiles_per_seq=n_t, tm=tm, d_model=d)
    ka_w = N_DIFF_HEADS * 2 * HEAD_WIDTH
    return pl.pallas_call(
        kern,
        grid=(t // tm,),
        in_specs=[
            pl.BlockSpec((tm, d), row),
            pl.BlockSpec(wq_t.shape, full2),
            pl.BlockSpec(wk.shape, full2),
            pl.BlockSpec(wv_t.shape, full2),
            pl.BlockSpec(wu.shape, full2),
            pl.BlockSpec(wg.shape, full2),
            pl.BlockSpec((1, 2 * d), full2),
            pl.BlockSpec(w_mix.shape, lambda i: (0, 0, 0)),
            pl.BlockSpec((1, POOL_WIDTH), full2),
            pl.BlockSpec((POOL_WIDTH, d), full2),
        ],
        out_specs=[
            pl.BlockSpec((None, None, ATTN_WIDTH, tm), blk4),
            pl.BlockSpec((None, None, tm, ka_w), blk4),
            pl.BlockSpec((None, None, ATTN_WIDTH, tm), blk4),
            pl.BlockSpec((tm, d), row),
            pl.BlockSpec((tm, d), row),
        ],
        out_shape=[
            jax.ShapeDtypeStruct((bsz, n_t, ATTN_WIDTH, tm), BF16),
            jax.ShapeDtypeStruct((bsz, n_t, tm, ka_w), BF16),
            jax.ShapeDtypeStruct((bsz, n_t, ATTN_WIDTH, tm), BF16),
            jax.ShapeDtypeStruct((t, d), BF16),
            jax.ShapeDtypeStruct((t, d), BF16),
        ],
        scratch_shapes=[pltpu.VMEM((POOL_HALO + tm, POOL_WIDTH), F32)],
        compiler_params=_params(),
        name="inproj",
    )(x2, wq_t, wk, wv_t, wu, wg, b_gate, w_mix, pool_scale, w_pool_br)


def _attn_kernel(lvec_ref, sg_ref, qt_ref, ka_ref, vt_ref, o_ref, m_scr, acc_scr, za_scr, zb_scr,
                 *, tq, lam_init):
    qi = pl.program_id(2)
    hd = DIFF_HEAD_DIM
    qt = qt_ref[...]
    sub = lax.broadcasted_iota(I32, (hd, tq), 0)
    ones_rows = jnp.where(sub < N_BIAS_PIECES, 1.0, 0.0).astype(BF16)
    qa = (jnp.concatenate([qt[0:hd], ones_rows], axis=0), jnp.concatenate([qt[hd:2 * hd], ones_rows], axis=0))

    m_scr[...] = jnp.full(m_scr.shape, -jnp.inf, F32)
    acc_scr[...] = jnp.zeros(acc_scr.shape, F32)
    sum_rows = jnp.ones((SUM_ROWS, tq), BF16)

    def scores(ki, z_ref):
        for m in range(2):
            ka = ka_ref[ki, :, m * HEAD_WIDTH:(m + 1) * HEAD_WIDTH]
            z_ref[m] = jnp.dot(ka, qa[m], preferred_element_type=F32)

    def consume(ki, z_ref, masked):
        vt = jnp.concatenate([vt_ref[ki], sum_rows], axis=0)
        for m in range(2):
            z = z_ref[m]
            if masked:
                r = lax.broadcasted_iota(I32, z.shape, 0)
                c = lax.broadcasted_iota(I32, z.shape, 1)
                z = jnp.where(r <= c, z, -jnp.inf)
            m_old = m_scr[m]
            m_new = jnp.maximum(m_old, jnp.max(z, axis=0, keepdims=True))
            alpha = jnp.exp2(m_old - m_new)
            p = jnp.exp2(z - m_new).astype(BF16)
            acc_scr[m] = alpha * acc_scr[m] + jnp.dot(vt, p, preferred_element_type=F32)
            m_scr[m] = m_new

    scores(0, za_scr)

    def pair(j, carry):
        ki = 2 * j
        scores(ki + 1, zb_scr)
        consume(ki, za_scr, False)
        scores(ki + 2, za_scr)
        consume(ki + 1, zb_scr, False)
        return carry

    lax.fori_loop(0, qi // 2, pair, 0)

    @pl.when(qi % 2 == 0)
    def _():
        consume(qi, za_scr, True)

    @pl.when(qi % 2 == 1)
    def _():
        scores(qi, zb_scr)
        consume(qi - 1, za_scr, False)
        consume(qi, zb_scr, True)

    lv = lvec_ref[...]
    lam = (jnp.exp(jnp.sum(lv[0:1] * lv[1:2], axis=-1, keepdims=True))
           - jnp.exp(jnp.sum(lv[2:3] * lv[3:4], axis=-1, keepdims=True)) + lam_init)
    hw = HEAD_WIDTH
    o = (acc_scr[0, 0:hw] / acc_scr[0, hw:hw + 1]
         - lam * (acc_scr[1, 0:hw] / acc_scr[1, hw:hw + 1]))
    ms = jnp.mean(o * o, axis=0, keepdims=True)
    o = o * lax.rsqrt(ms + LN_EPS) * sg_ref[...] * (1.0 - lam_init)
    o_ref[...] = o.T.astype(BF16)


def _attention(qt, kaug, vt, lvec, subln_g_col, lam_init):
    b, n_t, _, tq = qt.shape
    kern = functools.partial(_attn_kernel, tq=tq, lam_init=lam_init)
    return pl.pallas_call(
        kern,
        grid=(b, N_DIFF_HEADS, n_t),
        in_specs=[
            pl.BlockSpec(lvec.shape, lambda bi, h, i: (0, 0)),
            pl.BlockSpec((HEAD_WIDTH, 1), lambda bi, h, i: (0, 0)),
            pl.BlockSpec((None, None, HEAD_WIDTH, tq), lambda bi, h, i: (bi, i, h, 0)),
            pl.BlockSpec((None, n_t, tq, 2 * HEAD_WIDTH), lambda bi, h, i: (bi, 0, 0, h)),
            pl.BlockSpec((None, n_t, HEAD_WIDTH, tq), lambda bi, h, i: (bi, 0, h, 0)),
        ],
        out_specs=pl.BlockSpec((None, tq, HEAD_WIDTH), lambda bi, h, i: (bi, i, h)),
        out_shape=jax.ShapeDtypeStruct((b, n_t * tq, ATTN_WIDTH), BF16),
        scratch_shapes=[
            pltpu.VMEM((2, 1, tq), F32),
            pltpu.VMEM((2, HEAD_WIDTH + SUM_ROWS, tq), F32),
            pltpu.VMEM((2, tq, tq), F32),
            pltpu.VMEM((2, tq, tq), F32),
        ],
        compiler_params=_params(3),
        name="diffattn",
    )(lvec, subln_g_col, qt, kaug, vt)


def _mix_kernel(o_ref, ga_ref, gp_ref, x_ref, wabr_ref, wout_ref, g1_ref, b1_ref, wrh_ref, wrl_ref, br_ref,
                x1_ref, x1t_ref, topi_ref, rank_ref, gate_ref, cnt_ref, carry, *, tm, alpha):
    i = pl.program_id(0)

    @pl.when(i == 0)
    def _():
        carry[...] = jnp.zeros(carry.shape, F32)

    a_branch = jnp.dot(o_ref[...], wabr_ref[...], preferred_element_type=F32)
    mixed = ga_ref[...].astype(F32) * a_branch + gp_ref[...].astype(F32)
    y = jnp.dot(mixed.astype(BF16), wout_ref[...], preferred_element_type=F32)
    x1 = _layer_norm(alpha * x_ref[...] + y, g1_ref[...], b1_ref[...])
    x1_ref[...] = x1
    _store_token_tiles(x1t_ref, x1)

    x_hi = x1.astype(BF16)
    x_lo = (x1 - x_hi.astype(F32)).astype(BF16)
    logits = (jnp.dot(x_hi, wrh_ref[...], preferred_element_type=F32)
              + jnp.dot(x_lo, wrh_ref[...], preferred_element_type=F32)
              + jnp.dot(x_hi, wrl_ref[...], preferred_element_type=F32)) + br_ref[...]
    e_iota = lax.broadcasted_iota(I32, logits.shape, 1).astype(F32)
    vals, idxs = [], []
    rem = logits
    for _ in range(TOP_K):
        m = jnp.max(rem, axis=-1, keepdims=True)
        idx = jnp.min(jnp.where(rem == m, e_iota, float(N_EXPERTS)), axis=-1, keepdims=True)
        vals.append(m)
        idxs.append(idx)
        rem = jnp.where(e_iota == idx, -jnp.inf, rem)
    exps = [jnp.exp(vk - vals[0]) for vk in vals]
    den = exps[0] + exps[1] + exps[2] + exps[3]

    sel = jnp.zeros(logits.shape, F32)
    for idx in idxs:
        sel = sel + jnp.where(e_iota == idx, 1.0, 0.0)
    r = lax.broadcasted_iota(I32, (tm, tm), 0)
    c = lax.broadcasted_iota(I32, (tm, tm), 1)
    tri = jnp.where(c < r, 1.0, 0.0).astype(BF16)
    prefix = jnp.dot(tri, sel.astype(BF16), preferred_element_type=F32) + carry[...]
    carry[...] = carry[...] + jnp.sum(sel, axis=0, keepdims=True)
    cnt_ref[...] = carry[...]

    lane = lax.broadcasted_iota(I32, (tm, TOP_K), 1)
    topi = jnp.zeros((tm, TOP_K), F32)
    rank = jnp.zeros((tm, TOP_K), F32)
    gate = jnp.zeros((tm, TOP_K), F32)
    for kk in range(TOP_K):
        rk = jnp.sum(jnp.where(e_iota == idxs[kk], prefix, 0.0), axis=-1, keepdims=True)
        topi = jnp.where(lane == kk, idxs[kk], topi)
        rank = jnp.where(lane == kk, rk, rank)
        gate = jnp.where(lane == kk, exps[kk] / den, gate)
    topi_ref[...] = topi.astype(I32)
    rank_ref[...] = rank.astype(I32)
    gate_ref[...] = gate


def _mix(o2, ga, gp, x2, w_attn_br, w_out, ln_g, ln_b, w_router_hi, w_router_lo, b_router, alpha):
    t, d = x2.shape
    tm = TOKEN_TILE
    row = lambda i: (i, 0)
    full2 = lambda i: (0, 0)
    kern = functools.partial(_mix_kernel, tm=tm, alpha=alpha)
    return pl.pallas_call(
        kern,
        grid=(t // tm,),
        in_specs=[
            pl.BlockSpec((tm, ATTN_WIDTH), row),
            pl.BlockSpec((tm, d), row),
            pl.BlockSpec((tm, d), row),
            pl.BlockSpec((tm, d), row),
            pl.BlockSpec((ATTN_WIDTH, d), full2),
            pl.BlockSpec((d, d), full2),
            pl.BlockSpec((1, d), full2),
            pl.BlockSpec((1, d), full2),
            pl.BlockSpec((d, N_EXPERTS), full2),
            pl.BlockSpec((d, N_EXPERTS), full2),
            pl.BlockSpec((1, N_EXPERTS), full2),
        ],
        out_specs=[
            pl.BlockSpec((tm, d), row),
            pl.BlockSpec((tm * ROW_TILE, LANES), row),
            pl.BlockSpec((tm, TOP_K), row),
            pl.BlockSpec((tm, TOP_K), row),
            pl.BlockSpec((tm, TOP_K), row),
            pl.BlockSpec((1, N_EXPERTS), full2),
        ],
        out_shape=[
            jax.ShapeDtypeStruct((t, d), F32),
            jax.ShapeDtypeStruct((t * ROW_TILE, LANES), F32),
            jax.ShapeDtypeStruct((t, TOP_K), I32),
            jax.ShapeDtypeStruct((t, TOP_K), I32),
            jax.ShapeDtypeStruct((t, TOP_K), F32),
            jax.ShapeDtypeStruct((1, N_EXPERTS), F32),
        ],
        scratch_shapes=[pltpu.VMEM((1, N_EXPERTS), F32)],
        compiler_params=_params(),
        name="mix_ln1_router",
    )(o2, ga, gp, x2, w_attn_br, w_out, ln_g, ln_b, w_router_hi, w_router_lo, b_router)


def _store_token_tiles(ref, val):
    rows = val.shape[0]
    for s in range(ROW_TILE):
        ref[pl.ds(s, rows, stride=ROW_TILE), :] = val[:, s * LANES:(s + 1) * LANES]


def _load_token_tiles(ref, rows):
    return jnp.concatenate([ref[pl.ds(s, rows, stride=ROW_TILE), :] for s in range(ROW_TILE)], axis=-1)


def _tile_copy(src_ref, src_row8, dst_ref, dst_row8, sem):
    return pltpu.make_async_copy(src_ref.at[pl.ds(pl.multiple_of(src_row8, ROW_TILE), ROW_TILE), :],
                                 dst_ref.at[pl.ds(pl.multiple_of(dst_row8, ROW_TILE), ROW_TILE), :], sem)


def _dispatch_kernel(dest_ref, unused_ref, x1t_ref, x1_ref, p_ref, wg_ref, wp_ref, xs_hbm, ple_ref, sem, fill_sem,
                     fill_src, *, tm):
    n_unused = unused_ref.shape[0]

    @pl.when(pl.program_id(0) == 0)
    def _():
        fill_src[...] = x1t_ref[0:ROW_TILE, :]

        def fill(r, carry):
            _tile_copy(fill_src, 0, xs_hbm, unused_ref[r], fill_sem).start(priority=1)
            return carry

        lax.fori_loop(0, n_unused, fill, 0, unroll=ISSUE_UNROLL)

    @pl.when(pl.program_id(0) == pl.num_programs(0) - 1)
    def _():
        for _ in range(n_unused // tm):
            pltpu.make_async_copy(x1t_ref, xs_hbm.at[pl.ds(0, tm * ROW_TILE), :], fill_sem).wait()

    def body(t, carry):
        for kk in range(TOP_K):
            _tile_copy(x1t_ref, t * ROW_TILE, xs_hbm, dest_ref[TOP_K * t + kk], sem).start(priority=kk % 2)
        return carry

    lax.fori_loop(0, tm, body, 0, unroll=ISSUE_UNROLL)
    ple_ref[...] = (_sigmoid(jnp.dot(x1_ref[...].astype(BF16), wg_ref[...], preferred_element_type=F32))
                    * jnp.dot(p_ref[...].astype(BF16), wp_ref[...], preferred_element_type=F32))
    for _ in range(TOP_K):
        pltpu.make_async_copy(x1t_ref, xs_hbm.at[pl.ds(0, tm * ROW_TILE), :], sem).wait()


def _dispatch(dest8, unused8, x1t, x1, p2, w_ple_gate, w_ple_proj, n_slots):
    tm = TOKEN_TILE
    t, d = x1.shape
    ple_dim = p2.shape[1]
    row = lambda i: (i, 0)
    full2 = lambda i: (0, 0)
    kern = functools.partial(_dispatch_kernel, tm=tm)
    return pl.pallas_call(
        kern,
        grid=(t // tm,),
        in_specs=[
            pl.BlockSpec((TOP_K * tm,), lambda i: (i,), memory_space=pltpu.SMEM),
            pl.BlockSpec(unused8.shape, lambda i: (0,), memory_space=pltpu.SMEM),
            pl.BlockSpec((tm * ROW_TILE, LANES), row),
            pl.BlockSpec((tm, d), row),
            pl.BlockSpec((tm, ple_dim), row),
            pl.BlockSpec((d, d), full2),
            pl.BlockSpec((ple_dim, d), full2),
        ],
        out_specs=[pl.BlockSpec(memory_space=pl.ANY), pl.BlockSpec((tm, d), row)],
        out_shape=[jax.ShapeDtypeStruct((n_slots * ROW_TILE, LANES), F32), jax.ShapeDtypeStruct((t, d), F32)],
        scratch_shapes=[pltpu.SemaphoreType.DMA(()), pltpu.SemaphoreType.DMA(()),
                        pltpu.VMEM((ROW_TILE, LANES), F32)],
        compiler_params=_params(),
        name="dispatch",
    )(dest8, unused8, x1t, x1, p2, w_ple_gate, w_ple_proj)


def _expert_kernel(be_ref, bsrc_ref, bv_ref, xs_ref, wup_ref, bup_ref, wdn_ref, bdn_ref, ys_ref,
                   wup_bf, wdn_bf, *, bm, d_ff):
    b = pl.program_id(0)
    nv = bv_ref[b]

    @pl.when(jnp.logical_or(b == 0, be_ref[b] != be_ref[jnp.maximum(b - 1, 0)]))
    def _():
        wup_bf[...] = wup_ref[0].astype(BF16)
        wdn_bf[...] = wdn_ref[0].astype(BF16)

    @pl.when(nv > 0)
    def _():
        rows = lax.broadcasted_iota(I32, (bm, 1), 0)
        x = jnp.where(rows < nv, _load_token_tiles(xs_ref, bm), 0.0).astype(BF16)
        hid = jnp.dot(x, wup_bf[...], preferred_element_type=F32) + bup_ref[0]
        glu = jnp.minimum(hid[:, :d_ff], SWIGLU_LIMIT)
        lin = jnp.clip(hid[:, d_ff:], -SWIGLU_LIMIT, SWIGLU_LIMIT)
        act = glu * _sigmoid(SWIGLU_ALPHA * glu) * (lin + 1.0)
        y = jnp.dot(act.astype(BF16), wdn_bf[...], preferred_element_type=F32) + bdn_ref[0]
        _store_token_tiles(ys_ref, y)

    @pl.when(nv == 0)
    def _():
        ys_ref[...] = jnp.zeros(ys_ref.shape, F32)


def _experts(be, bsrc, bv, xs, w_up, b_up, w_down, b_down):
    bm = EXPERT_BLOCK
    n_slots = xs.shape[0] // ROW_TILE
    d_ff, d = w_down.shape[1:]
    kern = functools.partial(_expert_kernel, bm=bm, d_ff=d_ff)
    blk = lambda b, be_r, bsrc_r, bv_r: (bsrc_r[b], 0)
    wsel = lambda b, be_r, bsrc_r, bv_r: (be_r[b], 0, 0)
    grid_spec = pltpu.PrefetchScalarGridSpec(
        num_scalar_prefetch=3,
        grid=(n_slots // bm,),
        in_specs=[
            pl.BlockSpec((bm * ROW_TILE, LANES), blk),
            pl.BlockSpec((1, d, 2 * d_ff), wsel),
            pl.BlockSpec((1, 1, 2 * d_ff), wsel),
            pl.BlockSpec((1, d_ff, d), wsel),
            pl.BlockSpec((1, 1, d), wsel),
        ],
        out_specs=pl.BlockSpec((bm * ROW_TILE, LANES), lambda b, be_r, bsrc_r, bv_r: (b, 0)),
        scratch_shapes=[pltpu.VMEM((d, 2 * d_ff), BF16), pltpu.VMEM((d_ff, d), BF16)],
    )
    return pl.pallas_call(
        kern,
        grid_spec=grid_spec,
        out_shape=jax.ShapeDtypeStruct((n_slots * ROW_TILE, LANES), F32),
        compiler_params=_params(),
        name="experts",
    )(be, bsrc, bv, xs, w_up, b_up, w_down, b_down)


def _final_kernel(dest_ref, dnext_ref, x1_ref, ple_ref, gate_ref, g2_ref, b2_ref, ys_hbm,
                  o_ref, ybuf, sem, *, tm, alpha):
    i = pl.program_id(0)
    slot = i % 2

    def gather(ids_ref, buf_slot):
        def body(t, carry):
            for kk in range(TOP_K):
                _tile_copy(ys_hbm, ids_ref[TOP_K * t + kk], ybuf.at[buf_slot, kk], t * ROW_TILE,
                           sem.at[buf_slot]).start(priority=kk % 2)
            return carry

        lax.fori_loop(0, tm, body, 0, unroll=ISSUE_UNROLL)

    @pl.when(i == 0)
    def _():
        gather(dest_ref, 0)

    @pl.when(i + 1 < pl.num_programs(0))
    def _():
        gather(dnext_ref, 1 - slot)

    for kk in range(TOP_K):
        pltpu.make_async_copy(ys_hbm.at[pl.ds(0, tm * ROW_TILE), :], ybuf.at[slot, kk], sem.at[slot]).wait()
    gate = gate_ref[...]
    ffn = gate[:, 0:1] * _load_token_tiles(ybuf.at[slot, 0], tm)
    for kk in range(1, TOP_K):
        ffn = ffn + gate[:, kk:kk + 1] * _load_token_tiles(ybuf.at[slot, kk], tm)
    o_ref[...] = _layer_norm(alpha * x1_ref[...] + ffn + ple_ref[...], g2_ref[...], b2_ref[...])


def _final(dest, x1, ple, gate, ln_g, ln_b, ys, alpha):
    t, d = x1.shape
    tm = COMBINE_TILE
    n_tiles = t // tm
    row = lambda i: (i, 0)
    full2 = lambda i: (0, 0)
    kern = functools.partial(_final_kernel, tm=tm, alpha=alpha)
    return pl.pallas_call(
        kern,
        grid=(n_tiles,),
        in_specs=[
            pl.BlockSpec((TOP_K * tm,), lambda i: (i,), memory_space=pltpu.SMEM),
            pl.BlockSpec((TOP_K * tm,), lambda i: (jnp.minimum(i + 1, n_tiles - 1),), memory_space=pltpu.SMEM),
            pl.BlockSpec((tm, d), row),
            pl.BlockSpec((tm, d), row),
            pl.BlockSpec((tm, TOP_K), row),
            pl.BlockSpec((1, d), full2),
            pl.BlockSpec((1, d), full2),
            pl.BlockSpec(memory_space=pl.ANY),
        ],
        out_specs=pl.BlockSpec((tm, d), row),
        out_shape=jax.ShapeDtypeStruct((t, d), F32),
        scratch_shapes=[pltpu.VMEM((2, TOP_K, tm * ROW_TILE, LANES), F32), pltpu.SemaphoreType.DMA((2,))],
        compiler_params=_params(),
        name="combine_ln2",
    )(dest, dest, x1, ple, gate, ln_g, ln_b, ys)


def _split_bf16(w):
    hi = lax.bitcast_convert_type(lax.bitcast_convert_type(w, jnp.uint32) & jnp.uint32(0xFFFF0000), F32)
    return hi.astype(BF16), (w - hi).astype(BF16)


def _routing_tables(counts, topi, rank, n_blocks):
    bm = EXPERT_BLOCK
    n_blk_e = (counts + bm - 1) // bm
    cum_blk = jnp.cumsum(n_blk_e)
    blk_start = cum_blk - n_blk_e
    n_used = cum_blk[-1]
    experts = jnp.arange(N_EXPERTS, dtype=I32)
    slot_start = jnp.sum(jnp.where(topi[..., None] == experts, blk_start * bm, 0), axis=-1)
    dest = (slot_start + rank).reshape(-1).astype(I32)
    bidx = jnp.arange(n_blocks, dtype=I32)
    bsrc = jnp.minimum(bidx, n_used - 1).astype(I32)
    be = jnp.minimum(jnp.sum((cum_blk[None, :] <= bsrc[:, None]).astype(I32), axis=-1), N_EXPERTS - 1)
    valid = jnp.clip(counts[be] - (bsrc - blk_start[be]) * bm, 0, bm)
    bv = jnp.where(bidx < n_used, valid, 0).astype(I32)
    seg_start = jnp.concatenate([blk_start * bm + counts, (n_used * bm)[None]])
    seg_size = jnp.concatenate([n_blk_e * bm - counts, ((n_blocks - n_used) * bm)[None]])
    seg_cum = jnp.cumsum(seg_size)
    j = jnp.arange(n_blocks * bm - dest.shape[0], dtype=I32)
    g = jnp.sum((seg_cum[None, :] <= j[:, None]).astype(I32), axis=-1)
    onehot = g[:, None] == jnp.arange(N_EXPERTS + 1, dtype=I32)[None, :]
    unused = (j + jnp.sum(jnp.where(onehot, seg_start - (seg_cum - seg_size), 0), axis=-1)).astype(I32)
    return dest, unused, be, bsrc, bv


def kernel(x, p, w_in, b_gate, lambda_q1, lambda_k1, lambda_q2, lambda_k2, subln_g, w_attn_br,
           w_pool_mix, pool_scale, w_pool_br, w_out, ln1_g, ln1_b, w_router, b_router, w_up, b_up,
           w_down, b_down, w_ple_gate, w_ple_proj, ln2_g, ln2_b):
    bsz, seq, d = x.shape
    depth = w_in.shape[0]
    t = bsz * seq
    assert seq % ATTN_TILE == 0 and seq % TOKEN_TILE == 0 and t % EXPERT_BLOCK == 0
    alpha = (2 * depth) ** 0.25
    n_blocks = (t * TOP_K) // EXPERT_BLOCK + N_EXPERTS
    n_slots = n_blocks * EXPERT_BLOCK

    x2 = x.reshape(t, d)
    for i in range(depth):
        lam_init = 0.8 - 0.6 * math.exp(-0.3 * i)
        wq, wk, wv, wu, wg = jnp.split(w_in[i].astype(BF16), IN_SPLITS, axis=-1)
        qt, kaug, vt, ga, gp = _inproj(x2, wq.T, wk, wv.T, wu, wg,
                                       b_gate[i][None], w_pool_mix[i].astype(BF16), pool_scale[i][None],
                                       w_pool_br[i].astype(BF16), bsz, seq)
        lvec = jnp.stack([lambda_q1[i], lambda_k1[i], lambda_q2[i], lambda_k2[i]]).astype(F32)
        o = _attention(qt, kaug, vt, lvec, subln_g[i][:, None], lam_init)
        x1, x1t, topi, rank, gate, cnt = _mix(o.reshape(t, ATTN_WIDTH), ga, gp, x2,
                                         w_attn_br[i].astype(BF16), w_out[i].astype(BF16),
                                         ln1_g[i][None], ln1_b[i][None], *_split_bf16(w_router[i]),
                                         b_router[i][None], alpha)
        counts = cnt[0].astype(I32)
        dest, unused, be, bsrc, bv = _routing_tables(counts, topi, rank, n_blocks)
        dest8 = dest * ROW_TILE
        xs, ple = _dispatch(dest8, unused * ROW_TILE, x1t, x1, p[i].reshape(t, -1),
                            w_ple_gate[i].astype(BF16), w_ple_proj[i].astype(BF16), n_slots)
        ys = _experts(be, bsrc, bv, xs, w_up[i], b_up[i][:, None, :], w_down[i], b_down[i][:, None, :])
        x2 = _final(dest8, x1, ple, gate, ln2_g[i][None], ln2_b[i][None], ys, alpha)
    return x2.reshape(bsz, seq, d)
```
